```python
import jax
import jax.numpy as jnp
from jax import lax
import numpy as np

D_MODEL = 2048
BATCH = 1
SEQ = 16384
DEPTH = 4
DEC_BATCH = 8
DEC_SEQ = 16
PAST_LEN = 1024

CHUNK = 64
N_MIXERS = 3
N_LAYERS_A = (DEPTH + 2) // N_MIXERS
N_LAYERS_B = (DEPTH + 1) // N_MIXERS
N_LAYERS_C = DEPTH // N_MIXERS
D_FF = 5504
NORM_EPS = 1e-6
HGRN_EXPAND = 128
HGRN_HEADS = D_MODEL // HGRN_EXPAND
HGRN_HEAD_V = D_MODEL // HGRN_HEADS
ATTN_HEADS = 32
ATTN_HEAD_DIM = D_MODEL // ATTN_HEADS
PREV_CHUNKS = 8
BAND = (PREV_CHUNKS + 1) * CHUNK
REL_CLIP = 128
RWKV_HEAD = 64
RWKV_HEADS = D_MODEL // RWKV_HEAD
RWKV_DECAY_LORA = 96
RWKV_A_LORA = 96
RWKV_GATE_LORA = 256
RWKV_GN_EPS = 64e-5

kernel_name = 'hybrid_streaming_encoder_step'


def rmsnorm(x, g):
    xf = x.astype(jnp.float32)
    y = xf * lax.rsqrt(jnp.mean(xf * xf, axis=-1, keepdims=True) + NORM_EPS)
    return (y * g.astype(jnp.float32)).astype(x.dtype)


def swiglu_half(x, w_gate_up, w_down):
    gate, up = jnp.split(x @ w_gate_up, 2, axis=-1)
    return 0.5 * ((jax.nn.silu(gate) * up) @ w_down)


def hgrn2_recurrence(q, k, v, log_f, s0, chunk):
    b, t, h, _ = q.shape
    dv = v.shape[-1]
    n = t // chunk

    def to_chunks(a):
        return a.astype(jnp.float32).reshape(b, n, chunk, h, a.shape[-1]).transpose(1, 0, 3, 2, 4)

    causal = jnp.tril(jnp.ones((chunk, chunk), dtype=bool))

    def step(s, inp):
        qc, kc, vc, gc = inp
        gcum = jnp.cumsum(gc, axis=2)
        rel = gcum[:, :, :, None, :] - gcum[:, :, None, :, :]
        decay = jnp.exp(jnp.where(causal[:, :, None], rel, -jnp.inf))
        scores = jnp.einsum('bhtk,bhsk,bhtsk->bhts', qc, kc, decay)
        o = (jnp.einsum('bhts,bhsv->bhtv', scores, vc)
             + jnp.einsum('bhtk,bhkv->bhtv', qc * jnp.exp(gcum), s))
        g_last = gcum[:, :, -1:, :]
        s_new = (jnp.exp(g_last[:, :, 0, :, None]) * s
                 + jnp.einsum('bhsk,bhsv->bhkv', kc * jnp.exp(g_last - gcum), vc))
        return s_new, o

    s_fin, o = lax.scan(step, s0.astype(jnp.float32),
                        (to_chunks(q), to_chunks(k), to_chunks(v), to_chunks(log_f)))
    o = o.transpose(1, 0, 3, 2, 4).reshape(b, t, h, dv)
    return o, s_fin


def hgrn2_mixer(h, s0, w, j, lower_bound):
    b, t, _ = h.shape
    q, fz, inp, gz = jnp.split(h @ w['hgrn_w_in'][j], 4, axis=-1)
    fz = fz.astype(jnp.float32)
    lb = lower_bound.astype(jnp.float32)
    q = jax.nn.silu(q.astype(jnp.float32)) * HGRN_EXPAND ** -0.5
    log_f = jnp.log(lb + (1.0 - lb) * jax.nn.sigmoid(fz))
    k = (1.0 - lb) * jax.nn.sigmoid(-fz)
    heads = lambda a: a.reshape(b, t, HGRN_HEADS, -1)
    chunk = min(CHUNK, t)
    o, s = hgrn2_recurrence(heads(q), heads(k), heads(inp), heads(log_f), s0, chunk)
    o = rmsnorm(o.reshape(b, t, D_MODEL), w['hgrn_norm_g'][j]) * jax.nn.silu(gz.astype(jnp.float32))
    return o.astype(h.dtype) @ w['hgrn_w_out'][j], s


def rel_bias_table(rel_bias, rel):
    return rel_bias[:, jnp.clip(rel, -REL_CLIP, REL_CLIP) + REL_CLIP].astype(jnp.float32)


def band_attention_prompt(q, k, v, rel_bias):
    b, t, h, dh = q.shape
    n = t // CHUNK
    past = PREV_CHUNKS * CHUNK
    pad = ((0, 0), (past, 0), (0, 0), (0, 0))
    kp, vp = jnp.pad(k, pad), jnp.pad(v, pad)
    kj = jnp.arange(BAND)
    bias = rel_bias_table(rel_bias, jnp.arange(CHUNK)[:, None] - kj[None, :] + past)
    qc = q.reshape(b, n, CHUNK, h, dh).transpose(1, 0, 2, 3, 4)

    def one_chunk(args):
        c, qb = args
        kb = lax.dynamic_slice_in_dim(kp, c * CHUNK, BAND, axis=1)
        vb = lax.dynamic_slice_in_dim(vp, c * CHUNK, BAND, axis=1)
        s = jnp.einsum('bqhd,bkhd->bhqk', qb, kb).astype(jnp.float32) * dh ** -0.5 + bias
        valid = kj >= past - c * CHUNK
        s = jnp.where(valid[None, None, None, :], s, -jnp.inf)
        p = jax.nn.softmax(s, axis=-1).astype(vb.dtype)
        return jnp.einsum('bhqk,bkhd->bqhd', p, vb)

    o = lax.map(one_chunk, (jnp.arange(n), qc))
    return o.transpose(1, 0, 2, 3, 4).reshape(b, t, h, dh)


def band_attention_cached(q, k_new, v_new, k_cache, v_cache, rel_bias):
    t, dh = q.shape[1], q.shape[-1]
    rows = k_cache.shape[1]
    k = jnp.concatenate([k_cache.astype(k_new.dtype), k_new], axis=1)
    v = jnp.concatenate([v_cache.astype(v_new.dtype), v_new], axis=1)
    bias = rel_bias_table(rel_bias, jnp.arange(t)[:, None] + rows - jnp.arange(rows + t)[None, :])
    s = jnp.einsum('bqhd,bkhd->bhqk', q, k).astype(jnp.float32) * dh ** -0.5 + bias
    p = jax.nn.softmax(s, axis=-1).astype(v.dtype)
    return jnp.einsum('bhqk,bkhd->bqhd', p, v)


def band_attention_mixer(h, k_cache, v_cache, w, j):
    b, t, _ = h.shape
    qkv = (h @ w['attn_w_qkv'][j]).reshape(b, t, 3, ATTN_HEADS, ATTN_HEAD_DIM)
    q, k, v = qkv[:, :, 0], qkv[:, :, 1], qkv[:, :, 2]
    rel_bias = w['attn_rel_bias'][j]
    if k_cache is None:
        o = band_attention_prompt(q, k, v, rel_bias)
        rows = min(PREV_CHUNKS * CHUNK, t)
        k_keep, v_keep = k[:, t - rows:], v[:, t - rows:]
    else:
        o = band_attention_cached(q, k, v, k_cache[j], v_cache[j], rel_bias)
        k_keep, v_keep = k, v
    return o.reshape(b, t, D_MODEL) @ w['attn_w_out'][j], k_keep, v_keep


def rwkv7_mixer(h, shift0, s0, w, j):
    b, t, d = h.shape
    mu = w['rwkv_mu'][j]
    x_prev = jnp.concatenate([shift0.astype(h.dtype), h[:, :-1]], axis=1)
    xx = x_prev - h
    lerp = lambda c: h + xx * mu[c]
    r = lerp(0) @ w['rwkv_w_rkv'][j, 0]
    k = lerp(1) @ w['rwkv_w_rkv'][j, 1]
    v = lerp(2) @ w['rwkv_w_rkv'][j, 2]
    w_log = -jax.nn.softplus(-(w['rwkv_w0'][j] + jnp.tanh(lerp(3) @ w['rwkv_w1'][j]) @ w['rwkv_w2'][j]).astype(jnp.float32)) - 0.5
    decay = jnp.exp(-jnp.exp(w_log))
    a = jax.nn.sigmoid((w['rwkv_a0'][j] + (lerp(4) @ w['rwkv_a1'][j]) @ w['rwkv_a2'][j]).astype(jnp.float32))
    gate = jax.nn.sigmoid(lerp(5) @ w['rwkv_g1'][j]) @ w['rwkv_g2'][j]
    heads = lambda z: z.astype(jnp.float32).reshape(b, t, RWKV_HEADS, RWKV_HEAD)
    kk = heads(k * w['rwkv_k_k'][j])
    kk = kk * lax.rsqrt(jnp.maximum(jnp.sum(kk * kk, axis=-1, keepdims=True), 1e-24))
    k = heads(k * (1.0 + (a - 1.0) * w['rwkv_k_a'][j]))
    r, v, decay, a = heads(r), heads(v), heads(decay), heads(a)

    def step(s, inp):
        r_t, k_t, v_t, w_t, kk_t, a_t = inp
        s = (s * w_t[:, :, None, :]
             - jnp.einsum('bhvk,bhk->bhv', s, kk_t)[..., None] * (kk_t * a_t)[:, :, None, :]
             + v_t[..., None] * k_t[:, :, None, :])
        return s, jnp.einsum('bhvk,bhk->bhv', s, r_t)

    tfirst = lambda z: jnp.swapaxes(z, 0, 1)
    s_fin, o = lax.scan(step, s0.astype(jnp.float32),
                        (tfirst(r), tfirst(k), tfirst(v), tfirst(decay), tfirst(kk), tfirst(a)))
    o = tfirst(o)
    mean = jnp.mean(o, axis=-1, keepdims=True)
    var = jnp.mean(jnp.square(o - mean), axis=-1, keepdims=True)
    o = ((o - mean) * lax.rsqrt(var + RWKV_GN_EPS)).reshape(b, t, d) * w['rwkv_ln_g'][j] + w['rwkv_ln_b'][j]
    bonus = jnp.sum(r * k * w['rwkv_r_k'][j], axis=-1, keepdims=True) * v
    o = (o + bonus.reshape(b, t, d)) * gate
    return o.astype(h.dtype) @ w['rwkv_w_out'][j], h[:, -1:], s_fin


def trunk(x, hgrn_s0, k_cache, v_cache, wkv_s0, shift0, w):
    probs = jax.nn.softmax(w['hgrn_lb_logits'].astype(jnp.float32), axis=0)
    lower = jnp.cumsum(probs, axis=0) - probs[0]
    hgrn_out, k_out, v_out, wkv_out, shift_out = [], [], [], [], []
    for layer in range(DEPTH):
        kind, j = layer % N_MIXERS, layer // N_MIXERS
        g = w['norm_g'][layer]
        x = x + swiglu_half(rmsnorm(x, g[0]), w['ffn_w_gate_up'][layer, 0], w['ffn_w_down'][layer, 0])
        h = rmsnorm(x, g[1])
        if kind == 0:
            y, s = hgrn2_mixer(h, hgrn_s0[j], w, j, lower[layer])
            hgrn_out.append(s)
        elif kind == 1:
            y, nk, nv = band_attention_mixer(h, k_cache, v_cache, w, j)
            k_out.append(nk)
            v_out.append(nv)
        else:
            y, sh, s = rwkv7_mixer(h, shift0[j], wkv_s0[j], w, j)
            shift_out.append(sh)
            wkv_out.append(s)
        x = x + y
        x = x + swiglu_half(rmsnorm(x, g[2]), w['ffn_w_gate_up'][layer, 1], w['ffn_w_down'][layer, 1])
    return (rmsnorm(x, w['final_norm_g']), jnp.stack(hgrn_out), jnp.stack(k_out), jnp.stack(v_out),
            jnp.stack(wkv_out), jnp.stack(shift_out))


def setup_inputs(seed: int = 0) -> dict:
    key = jax.random.key(seed)
    ks = iter(jax.random.split(key, 40))

    def rnd(shape, scale):
        return jax.random.normal(next(ks), shape, jnp.float32) * scale

    def dense(shape):
        return rnd(shape, shape[-2] ** -0.5)

    def gain(shape):
        return 1.0 + rnd(shape, 0.1)

    band_rows = min(PREV_CHUNKS * CHUNK, PAST_LEN)
    D = D_MODEL
    return {
        'x_prompt': rnd((BATCH, SEQ, D), 1.0),
        'x_sample': rnd((DEC_BATCH, DEC_SEQ, D), 1.0),
        'state_hgrn': rnd((N_LAYERS_A, DEC_BATCH, HGRN_HEADS, HGRN_EXPAND, HGRN_HEAD_V), 0.5),
        'cache_k_band': rnd((N_LAYERS_B, DEC_BATCH, band_rows, ATTN_HEADS, ATTN_HEAD_DIM), 1.0),
        'cache_v_band': rnd((N_LAYERS_B, DEC_BATCH, band_rows, ATTN_HEADS, ATTN_HEAD_DIM), 1.0),
        'state_wkv': rnd((N_LAYERS_C, DEC_BATCH, RWKV_HEADS, RWKV_HEAD, RWKV_HEAD), 0.5),
        'state_shift': rnd((N_LAYERS_C, DEC_BATCH, 1, D), 1.0),
        'norm_g': gain((DEPTH, 3, D)),
        'final_norm_g': gain((D,)),
        'ffn_w_gate_up': dense((DEPTH, 2, D, 2 * D_FF)),
        'ffn_w_down': dense((DEPTH, 2, D_FF, D)),
        'hgrn_w_in': dense((N_LAYERS_A, D, 4 * D)),
        'hgrn_lb_logits': rnd((DEPTH, D), 0.5),
        'hgrn_norm_g': gain((N_LAYERS_A, D)),
        'hgrn_w_out': dense((N_LAYERS_A, D, D)),
        'attn_w_qkv': dense((N_LAYERS_B, D, 3 * D)),
        'attn_rel_bias': rnd((N_LAYERS_B, ATTN_HEADS, 2 * REL_CLIP + 1), 0.5),
        'attn_w_out': dense((N_LAYERS_B, D, D)),
        'rwkv_mu': jax.random.uniform(next(ks), (N_LAYERS_C, 6, D), jnp.float32),
        'rwkv_w_rkv': dense((N_LAYERS_C, 3, D, D)),
        'rwkv_w0': rnd((N_LAYERS_C, D), 0.5),
        'rwkv_w1': dense((N_LAYERS_C, D, RWKV_DECAY_LORA)),
        'rwkv_w2': dense((N_LAYERS_C, RWKV_DECAY_LORA, D)),
        'rwkv_a0': rnd((N_LAYERS_C, D), 0.1),
        'rwkv_a1': dense((N_LAYERS_C, D, RWKV_A_LORA)),
        'rwkv_a2': dense((N_LAYERS_C, RWKV_A_LORA, D)),
        'rwkv_g1': dense((N_LAYERS_C, D, RWKV_GATE_LORA)),
        'rwkv_g2': dense((N_LAYERS_C, RWKV_GATE_LORA, D)),
        'rwkv_k_k': 0.85 + rnd((N_LAYERS_C, D), 0.1),
        'rwkv_k_a': gain((N_LAYERS_C, D)),
        'rwkv_r_k': rnd((N_LAYERS_C, RWKV_HEADS, RWKV_HEAD), 0.1),
        'rwkv_ln_g': gain((N_LAYERS_C, D)),
        'rwkv_ln_b': rnd((N_LAYERS_C, D), 0.02),
        'rwkv_w_out': dense((N_LAYERS_C, D, D)),
    }


def reference(x_prompt, x_sample, state_hgrn, cache_k_band, cache_v_band, state_wkv, state_shift,
              norm_g, final_norm_g, ffn_w_gate_up, ffn_w_down,
              hgrn_w_in, hgrn_lb_logits, hgrn_norm_g, hgrn_w_out,
              attn_w_qkv, attn_rel_bias, attn_w_out,
              rwkv_mu, rwkv_w_rkv, rwkv_w0, rwkv_w1, rwkv_w2, rwkv_a0, rwkv_a1, rwkv_a2,
              rwkv_g1, rwkv_g2, rwkv_k_k, rwkv_k_a, rwkv_r_k, rwkv_ln_g, rwkv_ln_b, rwkv_w_out):
    w = dict(norm_g=norm_g, final_norm_g=final_norm_g, ffn_w_gate_up=ffn_w_gate_up, ffn_w_down=ffn_w_down,
             hgrn_w_in=hgrn_w_in, hgrn_lb_logits=hgrn_lb_logits, hgrn_norm_g=hgrn_norm_g, hgrn_w_out=hgrn_w_out,
             attn_w_qkv=attn_w_qkv, attn_rel_bias=attn_rel_bias, attn_w_out=attn_w_out,
             rwkv_mu=rwkv_mu, rwkv_w_rkv=rwkv_w_rkv, rwkv_w0=rwkv_w0, rwkv_w1=rwkv_w1, rwkv_w2=rwkv_w2,
             rwkv_a0=rwkv_a0, rwkv_a1=rwkv_a1, rwkv_a2=rwkv_a2, rwkv_g1=rwkv_g1, rwkv_g2=rwkv_g2,
             rwkv_k_k=rwkv_k_k, rwkv_k_a=rwkv_k_a, rwkv_r_k=rwkv_r_k, rwkv_ln_g=rwkv_ln_g, rwkv_ln_b=rwkv_ln_b,
             rwkv_w_out=rwkv_w_out)
    b = x_prompt.shape[0]
    hgrn0 = jnp.zeros((N_LAYERS_A, b, HGRN_HEADS, HGRN_EXPAND, HGRN_HEAD_V), jnp.float32)
    wkv0 = jnp.zeros((N_LAYERS_C, b, RWKV_HEADS, RWKV_HEAD, RWKV_HEAD), jnp.float32)
    shift0 = jnp.zeros((N_LAYERS_C, b, 1, D_MODEL), x_prompt.dtype)
    y_prompt, hgrn_p, k_p, v_p, wkv_p, shift_p = trunk(x_prompt, hgrn0, None, None, wkv0, shift0, w)
    y_sample, hgrn_s, k_s, v_s, wkv_s, shift_s = trunk(x_sample, state_hgrn, cache_k_band, cache_v_band,
                                                       state_wkv, state_shift, w)
    return (y_prompt, y_sample, hgrn_p, hgrn_s, k_p, v_p, k_s, v_s, wkv_p, wkv_s, shift_p, shift_s)
```

```python
import functools
import math

import jax
import jax.numpy as jnp
from jax import lax
from jax.experimental import pallas as pl
from jax.experimental.pallas import tpu as pltpu

F32 = jnp.float32
MXU_DTYPE = jnp.bfloat16

LANES = 128
CHUNK = 64
PREV_CHUNKS = 8
BAND = (PREV_CHUNKS + 1) * CHUNK
REL_CLIP = 128
NORM_EPS = 1e-6
RWKV_GN_EPS = 64e-5
HGRN_HEAD = 128
ATTN_HEAD_DIM = 64
RWKV_HEAD = 64
SUB = 16
VMEM_LIMIT = 56 * 1024 * 1024


def _params(*sem):
    return pltpu.CompilerParams(dimension_semantics=sem, vmem_limit_bytes=VMEM_LIMIT)


def _tile(n, pref):
    if n <= pref:
        return n
    t = pref
    while n % t:
        t //= 2
    return t


def _mm(a, b):
    return jnp.dot(a.astype(MXU_DTYPE), b.astype(MXU_DTYPE), preferred_element_type=F32)


def _mm_nt(a, b):
    return lax.dot_general(a.astype(MXU_DTYPE), b.astype(MXU_DTYPE),
                           (((1,), (1,)), ((), ())), preferred_element_type=F32)


def _mm_tn(a, b):
    return lax.dot_general(a.astype(MXU_DTYPE), b.astype(MXU_DTYPE),
                           (((0,), (0,)), ((), ())), preferred_element_type=F32)


def _split3(x):
    hi = x.astype(jnp.bfloat16)
    r1 = x - hi.astype(F32)
    mid = r1.astype(jnp.bfloat16)
    lo = (r1 - mid.astype(F32)).astype(jnp.bfloat16)
    return hi, mid, lo


def _mm_sel(x, sel):
    sel = sel.astype(jnp.bfloat16)
    hi, mid, lo = _split3(x)
    d = lambda p: jnp.dot(p, sel, preferred_element_type=F32)
    return d(hi) + (d(mid) + d(lo))


def _sel_mm(sel, x):
    sel = sel.astype(jnp.bfloat16)
    hi, mid, lo = _split3(x)
    d = lambda p: jnp.dot(sel, p, preferred_element_type=F32)
    return d(hi) + (d(mid) + d(lo))


def _rms(x):
    return x * lax.rsqrt(jnp.mean(x * x, axis=-1, keepdims=True) + NORM_EPS)


def _sigmoid(x):
    return 1.0 / (1.0 + jnp.exp(-x))


def _silu(x):
    return x * _sigmoid(x)


def _ffn_kernel(x_ref, g_ref, wg_ref, wu_ref, wd_ref, o_ref, xn_ref, acc_ref):
    j = pl.program_id(1)

    @pl.when(j == 0)
    def _():
        xn_ref[...] = (_rms(x_ref[...]) * g_ref[...]).astype(xn_ref.dtype)
        acc_ref[...] = jnp.zeros_like(acc_ref)

    xn = xn_ref[...]
    gate = jnp.dot(xn, wg_ref[...], preferred_element_type=F32)
    up = jnp.dot(xn, wu_ref[...], preferred_element_type=F32)
    h = (_silu(gate) * up).astype(MXU_DTYPE)
    acc_ref[...] += jnp.dot(h, wd_ref[...], preferred_element_type=F32)

    @pl.when(j == pl.num_programs(1) - 1)
    def _():
        o_ref[...] = x_ref[...] + 0.5 * acc_ref[...]


def _ffn(x, g, w_gu, w_d, tf):
    t, d = x.shape
    fp = w_d.shape[0]
    nf = fp // tf
    tm = _tile(t, 512)
    return pl.pallas_call(
        _ffn_kernel,
        grid=(t // tm, nf),
        in_specs=[
            pl.BlockSpec((tm, d), lambda i, j: (i, 0)),
            pl.BlockSpec((1, d), lambda i, j: (0, 0)),
            pl.BlockSpec((d, tf), lambda i, j: (0, j)),
            pl.BlockSpec((d, tf), lambda i, j: (0, nf + j)),
            pl.BlockSpec((tf, d), lambda i, j: (j, 0)),
        ],
        out_specs=pl.BlockSpec((tm, d), lambda i, j: (i, 0)),
        out_shape=jax.ShapeDtypeStruct((t, d), F32),
        scratch_shapes=[pltpu.VMEM((tm, d), MXU_DTYPE), pltpu.VMEM((tm, d), F32)],
        compiler_params=_params("parallel", "arbitrary"),
        name="ffn",
    )(x, g.reshape(1, d), w_gu, w_gu, w_d)


def _prep_ffn_weights(w_gate_up, w_down):
    d, f2 = w_gate_up.shape
    f = f2 // 2
    tf = 512 if f > 512 else f
    fp = -(-f // tf) * tf
    pad = fp - f
    gate = jnp.pad(w_gate_up[:, :f], ((0, 0), (0, pad)))
    up = jnp.pad(w_gate_up[:, f:], ((0, 0), (0, pad)))
    w_gu = jnp.concatenate([gate, up], axis=1).astype(MXU_DTYPE)
    w_d = jnp.pad(w_down, ((0, pad), (0, 0))).astype(MXU_DTYPE)
    return w_gu, w_d, tf


def _norm_mm_kernel(x_ref, g_ref, w_ref, o_ref, xn_ref):
    @pl.when(pl.program_id(1) == 0)
    def _():
        xn_ref[...] = (_rms(x_ref[...]) * g_ref[...]).astype(xn_ref.dtype)

    o_ref[...] = jnp.dot(xn_ref[...], w_ref[...], preferred_element_type=F32)


def _norm_mm(x, g, w):
    t, d = x.shape
    n = w.shape[1]
    tm, tn = _tile(t, 512), _tile(n, 1024)
    return pl.pallas_call(
        _norm_mm_kernel,
        grid=(t // tm, n // tn),
        in_specs=[
            pl.BlockSpec((tm, d), lambda i, j: (i, 0)),
            pl.BlockSpec((1, d), lambda i, j: (0, 0)),
            pl.BlockSpec((d, tn), lambda i, j: (0, j)),
        ],
        out_specs=pl.BlockSpec((tm, tn), lambda i, j: (i, j)),
        out_shape=jax.ShapeDtypeStruct((t, n), F32),
        scratch_shapes=[pltpu.VMEM((tm, d), MXU_DTYPE)],
        compiler_params=_params("parallel", "arbitrary"),
        name="norm_mm",
    )(x, g.reshape(1, d), w)


def _norm_kernel(x_ref, g_ref, o_ref):
    o_ref[...] = _rms(x_ref[...]) * g_ref[...]


def _norm(x, g):
    t, d = x.shape
    tm = _tile(t, 512)
    return pl.pallas_call(
        _norm_kernel,
        grid=(t // tm,),
        in_specs=[pl.BlockSpec((tm, d), lambda i: (i, 0)), pl.BlockSpec((1, d), lambda i: (0, 0))],
        out_specs=pl.BlockSpec((tm, d), lambda i: (i, 0)),
        out_shape=jax.ShapeDtypeStruct((t, d), F32),
        compiler_params=_params("parallel"),
        name="norm",
    )(x, g.reshape(1, d))


def _proj_res_kernel(x_ref, a_ref, w_ref, o_ref):
    o_ref[...] = x_ref[...] + _mm(a_ref[...], w_ref[...])


def _gated_proj_res_kernel(x_ref, a_ref, z_ref, g_ref, w_ref, o_ref, an_ref):
    @pl.when(pl.program_id(1) == 0)
    def _():
        an = _rms(a_ref[...]) * g_ref[...] * _silu(z_ref[...])
        an_ref[...] = an.astype(an_ref.dtype)

    o_ref[...] = x_ref[...] + jnp.dot(an_ref[...], w_ref[...], preferred_element_type=F32)


def _proj_res(x, a, w):
    t, d = x.shape
    k = a.shape[1]
    tm, tn = _tile(t, 512), _tile(d, 1024)
    return pl.pallas_call(
        _proj_res_kernel,
        grid=(t // tm, d // tn),
        in_specs=[
            pl.BlockSpec((tm, tn), lambda i, j: (i, j)),
            pl.BlockSpec((tm, k), lambda i, j: (i, 0)),
            pl.BlockSpec((k, tn), lambda i, j: (0, j)),
        ],
        out_specs=pl.BlockSpec((tm, tn), lambda i, j: (i, j)),
        out_shape=jax.ShapeDtypeStruct((t, d), F32),
        compiler_params=_params("parallel", "arbitrary"),
        name="proj_res",
    )(x, a, w)


def _gated_proj_res(x, a, z_src, z_col, g, w):
    t, d = x.shape
    tm, tn = _tile(t, 512), _tile(d, 1024)
    return pl.pallas_call(
        _gated_proj_res_kernel,
        grid=(t // tm, d // tn),
        in_specs=[
            pl.BlockSpec((tm, tn), lambda i, j: (i, j)),
            pl.BlockSpec((tm, d), lambda i, j: (i, 0)),
            pl.BlockSpec((tm, d), lambda i, j: (i, z_col)),
            pl.BlockSpec((1, d), lambda i, j: (0, 0)),
            pl.BlockSpec((d, tn), lambda i, j: (0, j)),
        ],
        out_specs=pl.BlockSpec((tm, tn), lambda i, j: (i, j)),
        out_shape=jax.ShapeDtypeStruct((t, d), F32),
        scratch_shapes=[pltpu.VMEM((tm, d), MXU_DTYPE)],
        compiler_params=_params("parallel", "arbitrary"),
        name="gated_proj_res",
    )(x, a, z_src, g.reshape(1, d), w)


def _hgrn_kernel(q_ref, f_ref, v_ref, lb_ref, s0_ref, o_ref, sout_ref, st_ref, *, nsub):
    i = pl.program_id(2)

    @pl.when(i == 0)
    def _():
        st_ref[...] = s0_ref[0, 0]

    lb = lb_ref[...]
    rows = lax.broadcasted_iota(jnp.int32, (SUB, SUB), 0)
    cols = lax.broadcasted_iota(jnp.int32, (SUB, SUB), 1)
    tri = (cols <= rows).astype(F32)
    row_id = lax.broadcasted_iota(jnp.int32, (SUB, HGRN_HEAD), 0)
    scale = HGRN_HEAD ** -0.5

    def step(c, carry):
        r0 = pl.multiple_of(c * SUB, SUB)
        q = _silu(q_ref[pl.ds(r0, SUB), :]) * scale
        fz = f_ref[pl.ds(r0, SUB), :]
        v = v_ref[pl.ds(r0, SUB), :]
        log_f = jnp.log(lb + (1.0 - lb) * _sigmoid(fz))
        k = (1.0 - lb) * _sigmoid(-fz)
        gcum = _sel_mm(tri, log_f)
        o = jnp.zeros((SUB, HGRN_HEAD), F32)
        for s in range(SUB):
            rel = jnp.where(row_id >= s, gcum - gcum[s:s + 1, :], -jnp.inf)
            w = jnp.sum(q * k[s:s + 1, :] * jnp.exp(rel), axis=-1, keepdims=True)
            o = o + w * v[s:s + 1, :]
        st = st_ref[...]
        o = o + _mm_nt(q * jnp.exp(gcum), st)
        o_ref[pl.ds(r0, SUB), :] = o
        g_last = gcum[SUB - 1:SUB, :]
        st_ref[...] = st * jnp.exp(g_last) + _mm_tn(v, k * jnp.exp(g_last - gcum))
        return carry

    lax.fori_loop(0, nsub, step, 0)

    @pl.when(i == pl.num_programs(2) - 1)
    def _():
        sout_ref[0, 0] = st_ref[...]


def _hgrn_recurrence(proj, lb, s0_t, b, t):
    d = proj.shape[1] // 4
    h = d // HGRN_HEAD
    tb = _tile(t, 512)
    nb = t // tb
    kern = functools.partial(_hgrn_kernel, nsub=tb // SUB)
    row = lambda bi, hi, i: bi * nb + i
    return pl.pallas_call(
        kern,
        grid=(b, h, nb),
        in_specs=[
            pl.BlockSpec((tb, HGRN_HEAD), lambda bi, hi, i: (row(bi, hi, i), hi)),
            pl.BlockSpec((tb, HGRN_HEAD), lambda bi, hi, i: (row(bi, hi, i), h + hi)),
            pl.BlockSpec((tb, HGRN_HEAD), lambda bi, hi, i: (row(bi, hi, i), 2 * h + hi)),
            pl.BlockSpec((1, HGRN_HEAD), lambda bi, hi, i: (0, hi)),
            pl.BlockSpec((1, 1, HGRN_HEAD, HGRN_HEAD), lambda bi, hi, i: (bi, hi, 0, 0)),
        ],
        out_specs=[
            pl.BlockSpec((tb, HGRN_HEAD), lambda bi, hi, i: (row(bi, hi, i), hi)),
            pl.BlockSpec((1, 1, HGRN_HEAD, HGRN_HEAD), lambda bi, hi, i: (bi, hi, 0, 0)),
        ],
        out_shape=[jax.ShapeDtypeStruct((b * t, d), F32),
                   jax.ShapeDtypeStruct((b, h, HGRN_HEAD, HGRN_HEAD), F32)],
        scratch_shapes=[pltpu.VMEM((HGRN_HEAD, HGRN_HEAD), F32)],
        compiler_params=_params("parallel", "parallel", "arbitrary"),
        name="hgrn_recurrence",
    )(proj, proj, proj, lb.reshape(1, d), s0_t)


def _softmax_pv(parts):
    m = functools.reduce(jnp.maximum, [jnp.max(s, axis=-1, keepdims=True) for s, _ in parts])
    es = [jnp.exp(s - m) for s, _ in parts]
    den = functools.reduce(jnp.add, [jnp.sum(e, axis=-1, keepdims=True) for e in es])
    return functools.reduce(jnp.add, [_mm(e / den, v) for e, (_, v) in zip(es, parts)])


def _attn_prompt_kernel(q_ref, kp_ref, kc_ref, vp_ref, vc_ref, bias_ref, o_ref, k_scr, v_scr, *, tq):
    i = pl.program_id(1)
    k_scr[0:tq, :] = kp_ref[...].astype(k_scr.dtype)
    k_scr[tq:2 * tq, :] = kc_ref[...].astype(k_scr.dtype)
    v_scr[0:tq, :] = vp_ref[...].astype(v_scr.dtype)
    v_scr[tq:2 * tq, :] = vc_ref[...].astype(v_scr.dtype)
    lane = lax.broadcasted_iota(jnp.int32, (CHUNK, LANES), 1)
    head_a = lane < ATTN_HEAD_DIM
    col = lax.broadcasted_iota(jnp.int32, (CHUNK, BAND), 1)
    scale = ATTN_HEAD_DIM ** -0.5
    past = PREV_CHUNKS * CHUNK
    for j in range(tq // CHUNK):
        q = q_ref[j * CHUNK:(j + 1) * CHUNK, :]
        w0 = tq - past + j * CHUNK
        kw = k_scr[w0:w0 + BAND, :]
        vw = v_scr[w0:w0 + BAND, :]
        valid = col + ((i - 1) * tq + w0) >= 0
        outs = []
        for hd, sel in enumerate((head_a, jnp.logical_not(head_a))):
            s = _mm_nt(jnp.where(sel, q, 0.0), kw) * scale + bias_ref[hd]
            s = jnp.where(valid, s, -jnp.inf)
            outs.append(_softmax_pv([(s, vw)]))
        o_ref[j * CHUNK:(j + 1) * CHUNK, :] = jnp.where(head_a, outs[0], outs[1])


def _attn_prompt(qkv, bias, t):
    d = qkv.shape[1] // 3
    npair = d // LANES
    tq = _tile(t, 512)
    assert tq >= PREV_CHUNKS * CHUNK and tq % CHUNK == 0
    prev = lambda i: jnp.maximum(i - 1, 0)
    return pl.pallas_call(
        functools.partial(_attn_prompt_kernel, tq=tq),
        grid=(npair, t // tq),
        in_specs=[
            pl.BlockSpec((tq, LANES), lambda p, i: (i, p)),
            pl.BlockSpec((tq, LANES), lambda p, i: (prev(i), npair + p)),
            pl.BlockSpec((tq, LANES), lambda p, i: (i, npair + p)),
            pl.BlockSpec((tq, LANES), lambda p, i: (prev(i), 2 * npair + p)),
            pl.BlockSpec((tq, LANES), lambda p, i: (i, 2 * npair + p)),
            pl.BlockSpec((2, CHUNK, BAND), lambda p, i: (p, 0, 0)),
        ],
        out_specs=pl.BlockSpec((tq, LANES), lambda p, i: (i, p)),
        out_shape=jax.ShapeDtypeStruct((t, d), F32),
        scratch_shapes=[pltpu.VMEM((2 * tq, LANES), MXU_DTYPE), pltpu.VMEM((2 * tq, LANES), MXU_DTYPE)],
        compiler_params=_params("parallel", "arbitrary"),
        name="attn_prompt",
    )(qkv, qkv, qkv, qkv, qkv, bias)


def _attn_cached_kernel(q_ref, kn_ref, vn_ref, kc_ref, vc_ref, bc_ref, bn_ref, o_ref):
    q = q_ref[...]
    lane = lax.broadcasted_iota(jnp.int32, q.shape, 1)
    head_a = lane < ATTN_HEAD_DIM
    scale = ATTN_HEAD_DIM ** -0.5
    kc, vc, kn, vn = kc_ref[0], vc_ref[0], kn_ref[...], vn_ref[...]
    outs = []
    for hd, sel in enumerate((head_a, jnp.logical_not(head_a))):
        qm = jnp.where(sel, q, 0.0)
        s_c = _mm_nt(qm, kc) * scale + bc_ref[hd]
        s_n = _mm_nt(qm, kn) * scale + bn_ref[hd]
        outs.append(_softmax_pv([(s_c, vc), (s_n, vn)]))
    o_ref[...] = jnp.where(head_a, outs[0], outs[1])


def _attn_cached(qkv, k_cache, v_cache, bias_c, bias_n, b, t):
    d = qkv.shape[1] // 3
    npair = d // LANES
    rows = k_cache.shape[1]
    return pl.pallas_call(
        _attn_cached_kernel,
        grid=(b, npair),
        in_specs=[
            pl.BlockSpec((t, LANES), lambda bi, p: (bi, p)),
            pl.BlockSpec((t, LANES), lambda bi, p: (bi, npair + p)),
            pl.BlockSpec((t, LANES), lambda bi, p: (bi, 2 * npair + p)),
            pl.BlockSpec((1, rows, LANES), lambda bi, p: (bi, 0, p)),
            pl.BlockSpec((1, rows, LANES), lambda bi, p: (bi, 0, p)),
            pl.BlockSpec((2, t, rows), lambda bi, p: (p, 0, 0)),
            pl.BlockSpec((2, t, t), lambda bi, p: (p, 0, 0)),
        ],
        out_specs=pl.BlockSpec((t, LANES), lambda bi, p: (bi, p)),
        out_shape=jax.ShapeDtypeStruct((b * t, d), F32),
        compiler_params=_params("parallel", "parallel"),
        name="attn_cached",
    )(qkv, qkv, qkv, k_cache, v_cache, bias_c, bias_n)


def _rel_bias(rel_bias, rel):
    return rel_bias[:, jnp.clip(rel, -REL_CLIP, REL_CLIP) + REL_CLIP].astype(F32)


def _rwkv_rkv_kernel(h_ref, p_ref, mu_ref, w_ref, o_ref, l_ref):
    @pl.when(pl.program_id(2) == 0)
    def _():
        h = h_ref[...]
        l_ref[...] = (h + (p_ref[...] - h) * mu_ref[0]).astype(l_ref.dtype)

    o_ref[0] = jnp.dot(l_ref[...], w_ref[0], preferred_element_type=F32)


def _rwkv_rkv(h, h_prev, mu, w_rkv):
    t, d = h.shape
    tm, tn = _tile(t, 512), _tile(d, 1024)
    return pl.pallas_call(
        _rwkv_rkv_kernel,
        grid=(3, t // tm, d // tn),
        in_specs=[
            pl.BlockSpec((tm, d), lambda c, i, j: (i, 0)),
            pl.BlockSpec((tm, d), lambda c, i, j: (i, 0)),
            pl.BlockSpec((1, 1, d), lambda c, i, j: (c, 0, 0)),
            pl.BlockSpec((1, d, tn), lambda c, i, j: (c, 0, j)),
        ],
        out_specs=pl.BlockSpec((1, tm, tn), lambda c, i, j: (c, i, j)),
        out_shape=jax.ShapeDtypeStruct((3, t, d), F32),
        scratch_shapes=[pltpu.VMEM((tm, d), MXU_DTYPE)],
        compiler_params=_params("parallel", "parallel", "arbitrary"),
        name="rwkv_rkv",
    )(h, h_prev, mu.reshape(-1, 1, d), w_rkv)


def _softplus(y):
    return jnp.maximum(y, 0.0) + jnp.log1p(jnp.exp(-jnp.abs(y)))


def _rwkv_lora_kernel(h_ref, p_ref, mu_ref, w0_ref, w1_ref, w2_ref, a0_ref, a1_ref, a2_ref,
                      g1_ref, g2_ref, lw_ref, a_ref, gate_ref):
    h = h_ref[...]
    xx = p_ref[...] - h
    lerp = lambda c: h + xx * mu_ref[c]
    z = w0_ref[...] + _mm(jnp.tanh(_mm(lerp(0), w1_ref[...])), w2_ref[...])
    w_log = -_softplus(-z) - 0.5
    lw_ref[...] = -jnp.exp(w_log)
    a_ref[...] = _sigmoid(a0_ref[...] + _mm(_mm(lerp(1), a1_ref[...]), a2_ref[...]))
    gate_ref[...] = _mm(_sigmoid(_mm(lerp(2), g1_ref[...])), g2_ref[...])


def _rwkv_lora(h, h_prev, mu_wag, w0, w1, w2, a0, a1, a2, g1, g2):
    t, d = h.shape
    tm = _tile(t, 256)
    row = pl.BlockSpec((tm, d), lambda i: (i, 0))
    full = lambda a: pl.BlockSpec(a.shape, lambda i: (0,) * a.ndim)
    consts = [mu_wag.reshape(3, 1, d), w0.reshape(1, d), w1, w2, a0.reshape(1, d), a1, a2, g1, g2]
    return pl.pallas_call(
        _rwkv_lora_kernel,
        grid=(t // tm,),
        in_specs=[row, row] + [full(c) for c in consts],
        out_specs=[row, row, row],
        out_shape=[jax.ShapeDtypeStruct((t, d), F32)] * 3,
        compiler_params=_params("parallel"),
        name="rwkv_lora",
    )(h, h_prev, *consts)


def _rwkv_kernel(r_ref, k_ref, v_ref, lw_ref, a_ref, gate_ref, kkw_ref, kaw_ref, rkw_ref,
                 lng_ref, lnb_ref, s0_ref, o_ref, sout_ref, s_ref, *, c_rows, nchunk):
    i = pl.program_id(2)
    c2 = 2 * c_rows

    @pl.when(i == 0)
    def _():
        s_ref[...] = s0_ref[0, 0]

    lane = lax.broadcasted_iota(jnp.int32, (c_rows, LANES), 1)
    head_a = lane < RWKV_HEAD
    li = lax.broadcasted_iota(jnp.int32, (LANES, LANES), 0) // RWKV_HEAD
    lj = lax.broadcasted_iota(jnp.int32, (LANES, LANES), 1) // RWKV_HEAD
    same_head = (li == lj).astype(F32)
    ri = lax.broadcasted_iota(jnp.int32, (c_rows, c_rows), 0)
    rj = lax.broadcasted_iota(jnp.int32, (c_rows, c_rows), 1)
    tri = (rj <= ri).astype(F32)
    si = lax.broadcasted_iota(jnp.int32, (c2, c2), 0)
    sj = lax.broadcasted_iota(jnp.int32, (c2, c2), 1)
    same_blk = (si // c_rows) == (sj // c_rows)
    strict = jnp.logical_and(same_blk, (sj % c_rows) < (si % c_rows))
    incl = jnp.logical_and(same_blk, (sj % c_rows) <= (si % c_rows))
    eye = (si == sj).astype(F32)
    kkw, kaw, rkw = kkw_ref[...], kaw_ref[...], rkw_ref[...]
    lng, lnb = lng_ref[...], lnb_ref[...]
    inv_n = 1.0 / RWKV_HEAD

    def stack(x):
        return jnp.concatenate([jnp.where(head_a, x, 0.0), jnp.where(head_a, 0.0, x)], axis=0)

    def chunk(c, carry):
        r0 = pl.multiple_of(c * c_rows, c_rows)
        rows = pl.ds(r0, c_rows)
        r, kr, v = r_ref[rows, :], k_ref[rows, :], v_ref[rows, :]
        lw, a, gate = lw_ref[rows, :], a_ref[rows, :], gate_ref[rows, :]
        kk = kr * kkw
        kk = kk * lax.rsqrt(jnp.maximum(_mm_sel(kk * kk, same_head), 1e-24))
        k = kr * (1.0 + (a - 1.0) * kaw)
        cl = _sel_mm(tri, lw)
        e_neg = jnp.exp(-cl)
        al = -kk * jnp.exp(cl - lw)
        rt = r * jnp.exp(cl)
        lhs = jnp.concatenate([stack(al), stack(rt)], axis=0)
        rhs = jnp.concatenate([stack(a * kk * e_neg), stack(k * e_neg)], axis=0)
        aa = _mm_nt(lhs, rhs)
        n_ab = jnp.where(strict, aa[:c2, :c2], 0.0)
        a_ak = jnp.where(strict, aa[:c2, c2:], 0.0)
        a_rb = jnp.where(incl, aa[c2:, :c2], 0.0)
        a_rk = jnp.where(incl, aa[c2:, c2:], 0.0)
        inv = eye + n_ab
        pw = n_ab
        for _ in range(int(math.log2(c_rows)) - 1):
            pw = _mm(pw, pw)
            inv = inv + _mm(inv, pw)
        s0 = s_ref[...]
        ls = _mm_nt(lhs, s0)
        vs = stack(v)
        u = _mm(inv, ls[:c2] + _mm(a_ak, vs))
        o2 = ls[c2:] + _mm(a_rb, u) + _mm(a_rk, vs)
        o = o2[:c_rows] + o2[c_rows:]
        s_new = s0 + _mm_tn(jnp.concatenate([u, vs], axis=0), rhs)
        s_ref[...] = s_new * jnp.exp(cl[c_rows - 1:c_rows, :])
        mean = _mm_sel(o, same_head) * inv_n
        dlt = o - mean
        var = _mm_sel(dlt * dlt, same_head) * inv_n
        on = dlt * lax.rsqrt(var + RWKV_GN_EPS) * lng + lnb
        bonus = _mm_sel(r * k * rkw, same_head) * v
        o_ref[rows, :] = (on + bonus) * gate
        return carry

    lax.fori_loop(0, nchunk, chunk, 0)

    @pl.when(i == pl.num_programs(2) - 1)
    def _():
        sout_ref[0, 0] = s_ref[...]


def _rwkv_recurrence(rkv, lw, a, gate, kkw, kaw, rkw, lng, lnb, s0_blk, b, t):
    d = lw.shape[1]
    npair = d // LANES
    c_rows = min(CHUNK, t)
    tb = _tile(t, 512)
    nb = t // tb
    kern = functools.partial(_rwkv_kernel, c_rows=c_rows, nchunk=tb // c_rows)
    rowblk = pl.BlockSpec((tb, LANES), lambda bi, p, i: (bi * nb + i, p))
    rkvblk = lambda c: pl.BlockSpec((None, tb, LANES), lambda bi, p, i: (c, bi * nb + i, p))
    vec = pl.BlockSpec((1, LANES), lambda bi, p, i: (0, p))
    st = pl.BlockSpec((1, 1, LANES, LANES), lambda bi, p, i: (bi, p, 0, 0))
    return pl.pallas_call(
        kern,
        grid=(b, npair, nb),
        in_specs=[rkvblk(0), rkvblk(1), rkvblk(2), rowblk, rowblk, rowblk, vec, vec, vec, vec, vec, st],
        out_specs=[rowblk, st],
        out_shape=[jax.ShapeDtypeStruct((b * t, d), F32),
                   jax.ShapeDtypeStruct((b, npair, LANES, LANES), F32)],
        scratch_shapes=[pltpu.VMEM((LANES, LANES), F32)],
        compiler_params=_params("parallel", "parallel", "arbitrary"),
        name="rwkv_recurrence",
    )(rkv, rkv, rkv, lw, a, gate, kkw.reshape(1, d), kaw.reshape(1, d), rkw.reshape(1, d),
      lng.reshape(1, d), lnb.reshape(1, d), s0_blk)


def _to_blockdiag(s):
    b, h, n, _ = s.shape
    s = s.reshape(b, h // 2, 2, n, n)
    z = jnp.zeros_like(s[:, :, 0])
    top = jnp.concatenate([s[:, :, 0], z], axis=-1)
    bot = jnp.concatenate([z, s[:, :, 1]], axis=-1)
    return jnp.concatenate([top, bot], axis=-2)


def _from_blockdiag(sb):
    b, p, n2, _ = sb.shape
    n = n2 // 2
    return jnp.stack([sb[:, :, :n, :n], sb[:, :, n:, n:]], axis=2).reshape(b, 2 * p, n, n)


def _trunk(x3, hgrn_s0, k_cache, v_cache, wkv_s0, shift0, w, lower):
    b, t, d = x3.shape
    x = x3.reshape(b * t, d)
    depth = w['norm_g'].shape[0]
    hgrn_out, k_out, v_out, wkv_out, shift_out = [], [], [], [], []
    for layer in range(depth):
        kind, j = layer % 3, layer // 3
        g = w['norm_g'][layer]
        x = _ffn(x, g[0], *w['ffn'][layer][0])
        if kind == 0:
            proj = _norm_mm(x, g[1], w['hgrn_w_in'][j])
            s0_t = jnp.swapaxes(hgrn_s0[j], -1, -2)
            o, s_t = _hgrn_recurrence(proj, lower[layer], s0_t, b, t)
            hgrn_out.append(jnp.swapaxes(s_t, -1, -2))
            x = _gated_proj_res(x, o, proj, 3, w['hgrn_norm_g'][j], w['hgrn_w_out'][j])
        elif kind == 1:
            qkv = _norm_mm(x, g[1], w['attn_w_qkv'][j])
            rel_bias = w['attn_rel_bias'][j]
            nh = rel_bias.shape[0]
            if k_cache is None:
                past = PREV_CHUNKS * CHUNK
                bias = _rel_bias(rel_bias, jnp.arange(CHUNK)[:, None] - jnp.arange(BAND)[None, :] + past)
                o = jnp.concatenate([_attn_prompt(qkv[bi * t:(bi + 1) * t], bias, t) for bi in range(b)], axis=0)
                rows = min(past, t)
                keep = lambda a: a.reshape(b, t, nh, ATTN_HEAD_DIM)[:, t - rows:]
            else:
                rows = k_cache.shape[2]
                bias = _rel_bias(rel_bias, jnp.arange(t)[:, None] + rows - jnp.arange(rows + t)[None, :])
                o = _attn_cached(qkv, k_cache[j].reshape(b, rows, d), v_cache[j].reshape(b, rows, d),
                                 bias[:, :, :rows], bias[:, :, rows:], b, t)
                keep = lambda a: a.reshape(b, t, nh, ATTN_HEAD_DIM)
            k_out.append(keep(qkv[:, d:2 * d]))
            v_out.append(keep(qkv[:, 2 * d:]))
            x = _proj_res(x, o, w['attn_w_out'][j])
        else:
            h = _norm(x, g[1])
            h3 = h.reshape(b, t, d)
            h_prev = jnp.concatenate([shift0[j].astype(F32), h3[:, :-1]], axis=1).reshape(b * t, d)
            mu = w['rwkv_mu'][j]
            rkv = _rwkv_rkv(h, h_prev, mu[:3], w['rwkv_w_rkv'][j])
            lw, a, gate = _rwkv_lora(h, h_prev, mu[3:], w['rwkv_w0'][j], w['rwkv_w1'][j], w['rwkv_w2'][j],
                                     w['rwkv_a0'][j], w['rwkv_a1'][j], w['rwkv_a2'][j],
                                     w['rwkv_g1'][j], w['rwkv_g2'][j])
            o, s_blk = _rwkv_recurrence(rkv, lw, a, gate, w['rwkv_k_k'][j], w['rwkv_k_a'][j],
                                        w['rwkv_r_k'][j].reshape(-1), w['rwkv_ln_g'][j], w['rwkv_ln_b'][j],
                                        _to_blockdiag(wkv_s0[j]), b, t)
            shift_out.append(h3[:, -1:])
            wkv_out.append(_from_blockdiag(s_blk))
            x = _proj_res(x, o, w['rwkv_w_out'][j])
        x = _ffn(x, g[2], *w['ffn'][layer][1])
    y = _norm(x, w['final_norm_g']).reshape(b, t, d)
    return (y, jnp.stack(hgrn_out), jnp.stack(k_out), jnp.stack(v_out),
            jnp.stack(wkv_out), jnp.stack(shift_out))


def kernel(x_prompt, x_sample, state_hgrn, cache_k_band, cache_v_band, state_wkv, state_shift,
           norm_g, final_norm_g, ffn_w_gate_up, ffn_w_down,
           hgrn_w_in, hgrn_lb_logits, hgrn_norm_g, hgrn_w_out,
           attn_w_qkv, attn_rel_bias, attn_w_out,
           rwkv_mu, rwkv_w_rkv, rwkv_w0, rwkv_w1, rwkv_w2, rwkv_a0, rwkv_a1, rwkv_a2,
           rwkv_g1, rwkv_g2, rwkv_k_k, rwkv_k_a, rwkv_r_k, rwkv_ln_g, rwkv_ln_b, rwkv_w_out):
    depth = norm_g.shape[0]
    cast = lambda a: a.astype(MXU_DTYPE)
    ffn = [[_prep_ffn_weights(ffn_w_gate_up[l, i], ffn_w_down[l, i]) for i in range(2)] for l in range(depth)]
    w = dict(norm_g=norm_g, final_norm_g=final_norm_g, ffn=ffn,
             hgrn_w_in=cast(hgrn_w_in), hgrn_norm_g=hgrn_norm_g, hgrn_w_out=cast(hgrn_w_out),
             attn_w_qkv=cast(attn_w_qkv), attn_rel_bias=attn_rel_bias, attn_w_out=cast(attn_w_out),
             rwkv_mu=rwkv_mu, rwkv_w_rkv=cast(rwkv_w_rkv), rwkv_w0=rwkv_w0, rwkv_w1=cast(rwkv_w1),
             rwkv_w2=cast(rwkv_w2), rwkv_a0=rwkv_a0, rwkv_a1=cast(rwkv_a1), rwkv_a2=cast(rwkv_a2),
             rwkv_g1=cast(rwkv_g1), rwkv_g2=cast(rwkv_g2), rwkv_k_k=rwkv_k_k, rwkv_k_a=rwkv_k_a,
             rwkv_r_k=rwkv_r_k, rwkv_ln_g=rwkv_ln_g, rwkv_ln_b=rwkv_ln_b, rwkv_w_out=cast(rwkv_w_out))
    probs = jax.nn.softmax(hgrn_lb_logits.astype(F32), axis=0)
    lower = jnp.cumsum(probs, axis=0) - probs[0]

    b = x_prompt.shape[0]
    d = x_prompt.shape[2]
    n_a, n_c = state_hgrn.shape[0], state_wkv.shape[0]
    hgrn0 = jnp.zeros((n_a, b) + state_hgrn.shape[2:], F32)
    wkv0 = jnp.zeros((n_c, b) + state_wkv.shape[2:], F32)
    shift0 = jnp.zeros((n_c, b, 1, d), F32)
    y_p, hgrn_p, k_p, v_p, wkv_p, shift_p = _trunk(x_prompt, hgrn0, None, None, wkv0, shift0, w, lower)
    y_s, hgrn_s, k_s, v_s, wkv_s, shift_s = _trunk(x_sample, state_hgrn, cache_k_band, cache_v_band,
                                                   state_wkv, state_shift, w, lower)
    return (y_p, y_s, hgrn_p, hgrn_s, k_p, v_p, k_s, v_s, wkv_p, wkv_s, shift_p, shift_s)
```

```python
import functools
import math

import numpy as np
import jax
import jax.numpy as jnp
from jax import lax
from jax.experimental import pallas as pl
from jax.experimental.pallas import tpu as pltpu

F32 = jnp.float32
MXU_DTYPE = jnp.bfloat16

LANES = 128
CHUNK = 64
PREV_CHUNKS = 8
BAND = (PREV_CHUNKS + 1) * CHUNK
REL_CLIP = 128
NORM_EPS = 1e-6
RWKV_GN_EPS = 64e-5
HGRN_HEAD = 128
HGRN_FINE_LEVELS = (8, 4)
ATTN_HEAD_DIM = 64
ATTN_CHUNKS_IN_FLIGHT = 4
RWKV_HEAD = 64
VMEM_LIMIT =56 * 1024 * 1024


def _params(*sem):
    return pltpu.CompilerParams(dimension_semantics=sem, vmem_limit_bytes=VMEM_LIMIT)


def _tile(n, pref):
    if n <= pref:
        return n
    t = pref
    while n % t:
        t //= 2
    return t


def _mm(a, b):
    return jnp.dot(a.astype(MXU_DTYPE), b.astype(MXU_DTYPE), preferred_element_type=F32)


def _mm_nt(a, b):
    return lax.dot_general(a.astype(MXU_DTYPE), b.astype(MXU_DTYPE),
                           (((1,), (1,)), ((), ())), preferred_element_type=F32)


def _mm_tn(a, b):
    return lax.dot_general(a.astype(MXU_DTYPE), b.astype(MXU_DTYPE),
                           (((0,), (0,)), ((), ())), preferred_element_type=F32)


def _split3(x):
    hi = x.astype(jnp.bfloat16)
    r1 = x - hi.astype(F32)
    mid = r1.astype(jnp.bfloat16)
    lo = (r1 - mid.astype(F32)).astype(jnp.bfloat16)
    return hi, mid, lo


def _sel_mm(sel, x):
    sel = sel.astype(jnp.bfloat16)
    hi, mid, lo = _split3(x)
    d = lambda p: jnp.dot(sel, p, preferred_element_type=F32)
    return d(hi) + (d(mid) + d(lo))


def _lockstep(gens):
    out = [None] * len(gens)
    live = list(range(len(gens)))
    while live:
        for n in list(live):
            try:
                next(gens[n])
            except StopIteration as stop:
                out[n] = stop.value
                live.remove(n)
    return out


def _rms(x):
    return x * lax.rsqrt(jnp.mean(x * x, axis=-1, keepdims=True) + NORM_EPS)


def _sigmoid(x):
    return 1.0 / (1.0 + jnp.exp(-x))


def _silu(x):
    return x * _sigmoid(x)


def _ffn_up_kernel(x_ref, g_ref, wg_ref, wu_ref, h_ref, xn_ref):
    @pl.when(pl.program_id(1) == 0)
    def _():
        xn_ref[...] = (_rms(x_ref[...]) * g_ref[...]).astype(xn_ref.dtype)

    xn = xn_ref[...]
    gate = jnp.dot(xn, wg_ref[...], preferred_element_type=F32)
    up = jnp.dot(xn, wu_ref[...], preferred_element_type=F32)
    h_ref[...] = (_silu(gate) * up).astype(h_ref.dtype)


def _ffn(x, g, w_gu, w_d, tf):
    t, d = x.shape
    fp = w_d.shape[0]
    nf = fp // tf
    tm = _tile(t, 1024)
    h = pl.pallas_call(
        _ffn_up_kernel,
        grid=(t // tm, nf),
        in_specs=[
            pl.BlockSpec((tm, d), lambda i, j: (i, 0)),
            pl.BlockSpec((1, d), lambda i, j: (0, 0)),
            pl.BlockSpec((d, tf), lambda i, j: (0, j)),
            pl.BlockSpec((d, tf), lambda i, j: (0, nf + j)),
        ],
        out_specs=pl.BlockSpec((tm, tf), lambda i, j: (i, j)),
        out_shape=jax.ShapeDtypeStruct((t, fp), MXU_DTYPE),
        scratch_shapes=[pltpu.VMEM((tm, d), MXU_DTYPE)],
        compiler_params=_params("parallel", "arbitrary"),
        name="ffn_up",
    )(x, g.reshape(1, d), w_gu, w_gu)
    return _proj_res(x, h, w_d, scale=0.5)


def _prep_ffn_weights(w_gate_up, w_down):
    d, f2 = w_gate_up.shape
    f = f2 // 2
    tf = 512 if f > 512 else f
    fp = -(-f // tf) * tf
    pad = fp - f
    gate = jnp.pad(w_gate_up[:, :f], ((0, 0), (0, pad)))
    up = jnp.pad(w_gate_up[:, f:], ((0, 0), (0, pad)))
    w_gu = jnp.concatenate([gate, up], axis=1).astype(MXU_DTYPE)
    w_d = jnp.pad(w_down, ((0, pad), (0, 0))).astype(MXU_DTYPE)
    return w_gu, w_d, tf


def _norm_mm_kernel(x_ref, g_ref, w_ref, o_ref, xn_ref):
    @pl.when(pl.program_id(1) == 0)
    def _():
        xn_ref[...] = (_rms(x_ref[...]) * g_ref[...]).astype(xn_ref.dtype)

    o_ref[...] = jnp.dot(xn_ref[...], w_ref[...], preferred_element_type=F32)


def _norm_mm(x, g, w):
    t, d = x.shape
    n = w.shape[1]
    tm, tn = _tile(t, 1024), _tile(n, 1024)
    return pl.pallas_call(
        _norm_mm_kernel,
        grid=(t // tm, n // tn),
        in_specs=[
            pl.BlockSpec((tm, d), lambda i, j: (i, 0)),
            pl.BlockSpec((1, d), lambda i, j: (0, 0)),
            pl.BlockSpec((d, tn), lambda i, j: (0, j)),
        ],
        out_specs=pl.BlockSpec((tm, tn), lambda i, j: (i, j)),
        out_shape=jax.ShapeDtypeStruct((t, n), F32),
        scratch_shapes=[pltpu.VMEM((tm, d), MXU_DTYPE)],
        compiler_params=_params("parallel", "arbitrary"),
        name="norm_mm",
    )(x, g.reshape(1, d), w)


def _norm_kernel(x_ref, g_ref, o_ref):
    o_ref[...] = _rms(x_ref[...]) * g_ref[...]


def _norm(x, g):
    t, d = x.shape
    tm = _tile(t, 512)
    return pl.pallas_call(
        _norm_kernel,
        grid=(t // tm,),
        in_specs=[pl.BlockSpec((tm, d), lambda i: (i, 0)), pl.BlockSpec((1, d), lambda i: (0, 0))],
        out_specs=pl.BlockSpec((tm, d), lambda i: (i, 0)),
        out_shape=jax.ShapeDtypeStruct((t, d), F32),
        compiler_params=_params("parallel"),
        name="norm",
    )(x, g.reshape(1, d))


def _proj_res_kernel(x_ref, a_ref, w_ref, o_ref, *, scale):
    y = _mm(a_ref[...], w_ref[...])
    o_ref[...] = x_ref[...] + (y if scale == 1.0 else scale * y)


def _gated_proj_res_kernel(x_ref, a_ref, z_ref, g_ref, w_ref, o_ref, an_ref):
    @pl.when(pl.program_id(1) == 0)
    def _():
        an = _rms(a_ref[...]) * g_ref[...] * _silu(z_ref[...])
        an_ref[...] = an.astype(an_ref.dtype)

    o_ref[...] = x_ref[...] + jnp.dot(an_ref[...], w_ref[...], preferred_element_type=F32)


def _proj_res(x, a, w, scale=1.0):
    t, d = x.shape
    k = a.shape[1]
    tm, tn = _tile(t, 512), _tile(d, 1024)
    return pl.pallas_call(
        functools.partial(_proj_res_kernel, scale=scale),
        grid=(t // tm, d // tn),
        in_specs=[
            pl.BlockSpec((tm, tn), lambda i, j: (i, j)),
            pl.BlockSpec((tm, k), lambda i, j: (i, 0)),
            pl.BlockSpec((k, tn), lambda i, j: (0, j)),
        ],
        out_specs=pl.BlockSpec((tm, tn), lambda i, j: (i, j)),
        out_shape=jax.ShapeDtypeStruct((t, d), F32),
        compiler_params=_params("parallel", "arbitrary"),
        name="proj_res",
    )(x, a, w)


def _gated_proj_res(x, a, z_src, z_col, g, w):
    t, d = x.shape
    tm, tn = _tile(t, 512), _tile(d, 1024)
    return pl.pallas_call(
        _gated_proj_res_kernel,
        grid=(t // tm, d // tn),
        in_specs=[
            pl.BlockSpec((tm, tn), lambda i, j: (i, j)),
            pl.BlockSpec((tm, d), lambda i, j: (i, 0)),
            pl.BlockSpec((tm, d), lambda i, j: (i, z_col)),
            pl.BlockSpec((1, d), lambda i, j: (0, 0)),
            pl.BlockSpec((d, tn), lambda i, j: (0, j)),
        ],
        out_specs=pl.BlockSpec((tm, tn), lambda i, j: (i, j)),
        out_shape=jax.ShapeDtypeStruct((t, d), F32),
        scratch_shapes=[pltpu.VMEM((tm, d), MXU_DTYPE)],
        compiler_params=_params("parallel", "arbitrary"),
        name="gated_proj_res",
    )(x, a, z_src, g.reshape(1, d), w)


def _hgrn_levels(rows):
    t = np.arange(rows)[:, None]
    r = np.arange(rows)[None, :]
    mats = [r <= t]
    for b in HGRN_FINE_LEVELS:
        mid = (t // b) * b + b // 2 - 1
        upper = (t % b) >= b // 2
        mats.append(np.where(upper, (r > mid) & (r <= t), (r > t) & (r <= mid)))
    return np.concatenate(mats, axis=0).astype(np.float32)


def _hgrn_kernel(q_ref, f_ref, v_ref, lb_ref, lvl_ref, s0_ref, o_ref, sout_ref, st_ref, *, rows, nblk):
    i = pl.program_id(2)

    @pl.when(i == 0)
    def _():
        st_ref[...] = s0_ref[0, 0]

    lb = lb_ref[...]
    lvl = lvl_ref[...]
    t_id = lax.broadcasted_iota(jnp.int32, (rows, HGRN_HEAD), 0)
    ti = lax.broadcasted_iota(jnp.int32, (rows, rows), 0)
    si = lax.broadcasted_iota(jnp.int32, (rows, rows), 1)
    scale = HGRN_HEAD ** -0.5

    def level(scores, b, qe, ke):
        upper = (t_id & (b - 1)) >= b // 2
        s_l = _mm_nt(jnp.where(upper, qe, 0.0), jnp.where(upper, 0.0, ke))
        if b < rows:
            sh = int(math.log2(b))
            s_l = jnp.where((ti >> sh) == (si >> sh), s_l, 0.0)
        return scores + s_l

    def block(sl):
        q = _silu(q_ref[sl, :]) * scale
        fz = f_ref[sl, :]
        v = v_ref[sl, :]
        f = lb + (1.0 - lb) * _sigmoid(fz)
        k = (1.0 - lb) * _sigmoid(-fz)
        sums = _sel_mm(lvl, jnp.log(f))
        yield
        g = sums[:rows]
        scores = jnp.where(ti == si, _mm_nt(q, k), 0.0)
        yield
        scores = level(scores, 2, q * f, k)
        yield
        for n, b in enumerate(HGRN_FINE_LEVELS):
            if b <= rows:
                e = jnp.exp(sums[(n + 1) * rows:(n + 2) * rows])
                scores = level(scores, b, q * e, k * e)
                yield
        b = 2 * HGRN_FINE_LEVELS[0]
        while b <= rows:
            g_mid = jnp.concatenate(
                [jnp.broadcast_to(g[m:m + 1, :], (b, HGRN_HEAD)) for m in range(b // 2 - 1, rows, b)], axis=0)
            upper = (t_id & (b - 1)) >= b // 2
            e = jnp.exp(jnp.where(upper, g - g_mid, g_mid - g))
            scores = level(scores, b, q * e, k * e)
            yield
            b *= 2
        g_last = g[rows - 1:rows, :]
        o_intra = _mm(scores, v)
        yield
        kv = _mm_tn(v, k * jnp.exp(g_last - g))
        return o_intra, q * jnp.exp(g), jnp.exp(g_last), kv

    slices = [slice(u * rows, (u + 1) * rows) for u in range(nblk)]
    st = st_ref[...]
    blocks = _lockstep([block(sl) for sl in slices])
    for sl, (o_intra, qg, dec, kv) in zip(slices, blocks):
        o_ref[sl, :] = o_intra + _mm_nt(qg, st)
        st = st * dec + kv
    st_ref[...] = st

    @pl.when(i == pl.num_programs(2) - 1)
    def _():
        sout_ref[0, 0] = st


def _hgrn_recurrence(proj, lb, s0_t, b, t):
    d = proj.shape[1] // 4
    h = d // HGRN_HEAD
    tb = _tile(t, 512)
    nb = t // tb
    rows = _tile(tb, 128)
    lvl = jnp.asarray(_hgrn_levels(rows), dtype=jnp.bfloat16)
    kern = functools.partial(_hgrn_kernel, rows=rows, nblk=tb // rows)
    row = lambda bi, hi, i: bi * nb + i
    return pl.pallas_call(
        kern,
        grid=(b, h, nb),
        in_specs=[
            pl.BlockSpec((tb, HGRN_HEAD), lambda bi, hi, i: (row(bi, hi, i), hi)),
            pl.BlockSpec((tb, HGRN_HEAD), lambda bi, hi, i: (row(bi, hi, i), h + hi)),
            pl.BlockSpec((tb, HGRN_HEAD), lambda bi, hi, i: (row(bi, hi, i), 2 * h + hi)),
            pl.BlockSpec((1, HGRN_HEAD), lambda bi, hi, i: (0, hi)),
            pl.BlockSpec(lvl.shape, lambda bi, hi, i: (0, 0)),
            pl.BlockSpec((1, 1, HGRN_HEAD, HGRN_HEAD), lambda bi, hi, i: (bi, hi, 0, 0)),
        ],
        out_specs=[
            pl.BlockSpec((tb, HGRN_HEAD), lambda bi, hi, i: (row(bi, hi, i), hi)),
            pl.BlockSpec((1, 1, HGRN_HEAD, HGRN_HEAD), lambda bi, hi, i: (bi, hi, 0, 0)),
        ],
        out_shape=[jax.ShapeDtypeStruct((b * t, d), F32),
                   jax.ShapeDtypeStruct((b, h, HGRN_HEAD, HGRN_HEAD), F32)],
        scratch_shapes=[pltpu.VMEM((HGRN_HEAD, HGRN_HEAD), F32)],
        compiler_params=_params("parallel", "parallel", "arbitrary"),
        name="hgrn_recurrence",
    )(proj, proj, proj, lb.reshape(1, d), lvl, s0_t)


def _softmax_pv(parts):
    m = functools.reduce(jnp.maximum, [jnp.max(s, axis=-1, keepdims=True) for s, _ in parts])
    es = [jnp.exp(s - m) for s, _ in parts]
    den = functools.reduce(jnp.add, [jnp.sum(e, axis=-1, keepdims=True) for e in es])
    yield
    pv = functools.reduce(jnp.add, [_mm(e, v) for e, (_, v) in zip(es, parts)])
    yield
    return pv / den


def _attn_prompt_kernel(q_ref, kp_ref, kc_ref, vp_ref, vc_ref, bias_ref, o_ref, k_scr, v_scr, *, tq):
    i = pl.program_id(1)
    k_scr[0:tq, :] = kp_ref[...].astype(k_scr.dtype)
    k_scr[tq:2 * tq, :] = kc_ref[...].astype(k_scr.dtype)
    v_scr[0:tq, :] = vp_ref[...].astype(v_scr.dtype)
    v_scr[tq:2 * tq, :] = vc_ref[...].astype(v_scr.dtype)
    lane = lax.broadcasted_iota(jnp.int32, (CHUNK, LANES), 1)
    head_a = lane < ATTN_HEAD_DIM
    col = lax.broadcasted_iota(jnp.int32, (CHUNK, BAND), 1)
    scale = ATTN_HEAD_DIM ** -0.5
    past = PREV_CHUNKS * CHUNK

    def unit(j, hd, masked):
        q = q_ref[j * CHUNK:(j + 1) * CHUNK, :] * scale
        w0 = tq - past + j * CHUNK
        sel = head_a if hd == 0 else jnp.logical_not(head_a)
        s = _mm_nt(jnp.where(sel, q, 0.0), k_scr[w0:w0 + BAND, :]) + bias_ref[hd]
        yield
        if masked:
            s = jnp.where(col + ((i - 1) * tq + w0) >= 0, s, -jnp.inf)
        return (yield from _softmax_pv([(s, v_scr[w0:w0 + BAND, :])]))

    def run(masked):
        for j0 in range(0, tq // CHUNK, ATTN_CHUNKS_IN_FLIGHT):
            js = range(j0, min(j0 + ATTN_CHUNKS_IN_FLIGHT, tq // CHUNK))
            outs = _lockstep([unit(j, hd, masked) for j in js for hd in range(2)])
            for n, j in enumerate(js):
                o_ref[j * CHUNK:(j + 1) * CHUNK, :] = jnp.where(head_a, outs[2 * n], outs[2 * n + 1])

    @pl.when(i == 0)
    def _():
        run(True)

    @pl.when(i > 0)
    def _():
        run(False)


def _attn_prompt(qkv, bias, t):
    d = qkv.shape[1] // 3
    npair = d // LANES
    tq = _tile(t, 512)
    assert tq >= PREV_CHUNKS * CHUNK and tq % CHUNK == 0
    prev = lambda i: jnp.maximum(i - 1, 0)
    return pl.pallas_call(
        functools.partial(_attn_prompt_kernel, tq=tq),
        grid=(npair, t // tq),
        in_specs=[
            pl.BlockSpec((tq, LANES), lambda p, i: (i, p)),
            pl.BlockSpec((tq, LANES), lambda p, i: (prev(i), npair + p)),
            pl.BlockSpec((tq, LANES), lambda p, i: (i, npair + p)),
            pl.BlockSpec((tq, LANES), lambda p, i: (prev(i), 2 * npair + p)),
            pl.BlockSpec((tq, LANES), lambda p, i: (i, 2 * npair + p)),
            pl.BlockSpec((2, CHUNK, BAND), lambda p, i: (p, 0, 0)),
        ],
        out_specs=pl.BlockSpec((tq, LANES), lambda p, i: (i, p)),
        out_shape=jax.ShapeDtypeStruct((t, d), F32),
        scratch_shapes=[pltpu.VMEM((2 * tq, LANES), MXU_DTYPE), pltpu.VMEM((2 * tq, LANES), MXU_DTYPE)],
        compiler_params=_params("parallel", "arbitrary"),
        name="attn_prompt",
    )(qkv, qkv, qkv, qkv, qkv, bias)


def _attn_cached_kernel(q_ref, kn_ref, vn_ref, kc_ref, vc_ref, bc_ref, bn_ref, o_ref):
    q = q_ref[...]
    lane = lax.broadcasted_iota(jnp.int32, q.shape, 1)
    head_a = lane < ATTN_HEAD_DIM
    scale = ATTN_HEAD_DIM ** -0.5
    kc, vc, kn, vn = kc_ref[0], vc_ref[0], kn_ref[...], vn_ref[...]

    def unit(hd, sel):
        qm = jnp.where(sel, q * scale, 0.0)
        s_c = _mm_nt(qm, kc) + bc_ref[hd]
        s_n = _mm_nt(qm, kn) + bn_ref[hd]
        return (yield from _softmax_pv([(s_c, vc), (s_n, vn)]))

    outs = _lockstep([unit(0, head_a), unit(1, jnp.logical_not(head_a))])
    o_ref[...] = jnp.where(head_a, outs[0], outs[1])


def _attn_cached(qkv, k_cache, v_cache, bias_c, bias_n, b, t):
    d = qkv.shape[1] // 3
    npair = d // LANES
    rows = k_cache.shape[1]
    return pl.pallas_call(
        _attn_cached_kernel,
        grid=(b, npair),
        in_specs=[
            pl.BlockSpec((t, LANES), lambda bi, p: (bi, p)),
            pl.BlockSpec((t, LANES), lambda bi, p: (bi, npair + p)),
            pl.BlockSpec((t, LANES), lambda bi, p: (bi, 2 * npair + p)),
            pl.BlockSpec((1, rows, LANES), lambda bi, p: (bi, 0, p)),
            pl.BlockSpec((1, rows, LANES), lambda bi, p: (bi, 0, p)),
            pl.BlockSpec((2, t, rows), lambda bi, p: (p, 0, 0)),
            pl.BlockSpec((2, t, t), lambda bi, p: (p, 0, 0)),
        ],
        out_specs=pl.BlockSpec((t, LANES), lambda bi, p: (bi, p)),
        out_shape=jax.ShapeDtypeStruct((b * t, d), F32),
        compiler_params=_params("parallel", "parallel"),
        name="attn_cached",
    )(qkv, qkv, qkv, k_cache, v_cache, bias_c, bias_n)


def _rel_bias(rel_bias, rel):
    return rel_bias[:, jnp.clip(rel, -REL_CLIP, REL_CLIP) + REL_CLIP].astype(F32)


def _rwkv_rkv_kernel(h_ref, p_ref, mu_ref, w_ref, o_ref, l_ref):
    @pl.when(pl.program_id(2) == 0)
    def _():
        h = h_ref[...]
        l_ref[...] = (h + (p_ref[...] - h) * mu_ref[0]).astype(l_ref.dtype)

    o_ref[0] = jnp.dot(l_ref[...], w_ref[0], preferred_element_type=F32)


def _rwkv_rkv(h, h_prev, mu, w_rkv):
    t, d = h.shape
    tm, tn = _tile(t, 512), _tile(d, 1024)
    return pl.pallas_call(
        _rwkv_rkv_kernel,
        grid=(3, t // tm, d // tn),
        in_specs=[
            pl.BlockSpec((tm, d), lambda c, i, j: (i, 0)),
            pl.BlockSpec((tm, d), lambda c, i, j: (i, 0)),
            pl.BlockSpec((1, 1, d), lambda c, i, j: (c, 0, 0)),
            pl.BlockSpec((1, d, tn), lambda c, i, j: (c, 0, j)),
        ],
        out_specs=pl.BlockSpec((1, tm, tn), lambda c, i, j: (c, i, j)),
        out_shape=jax.ShapeDtypeStruct((3, t, d), F32),
        scratch_shapes=[pltpu.VMEM((tm, d), MXU_DTYPE)],
        compiler_params=_params("parallel", "parallel", "arbitrary"),
        name="rwkv_rkv",
    )(h, h_prev, mu.reshape(-1, 1, d), w_rkv)


def _softplus(y):
    return jnp.maximum(y, 0.0) + jnp.log1p(jnp.exp(-jnp.abs(y)))


def _rwkv_lora_kernel(h_ref, p_ref, mu_ref, w0_ref, w1_ref, w2_ref, a0_ref, a1_ref, a2_ref,
                      g1_ref, g2_ref, lw_ref, a_ref, gate_ref):
    h = h_ref[...]
    xx = p_ref[...] - h
    lerp = lambda c: h + xx * mu_ref[c]
    z = w0_ref[...] + _mm(jnp.tanh(_mm(lerp(0), w1_ref[...])), w2_ref[...])
    w_log = -_softplus(-z) - 0.5
    lw_ref[...] = -jnp.exp(w_log)
    a_ref[...] = _sigmoid(a0_ref[...] + _mm(_mm(lerp(1), a1_ref[...]), a2_ref[...]))
    gate_ref[...] = _mm(_sigmoid(_mm(lerp(2), g1_ref[...])), g2_ref[...])


def _rwkv_lora(h, h_prev, mu_wag, w0, w1, w2, a0, a1, a2, g1, g2):
    t, d = h.shape
    tm = _tile(t, 256)
    row = pl.BlockSpec((tm, d), lambda i: (i, 0))
    full = lambda a: pl.BlockSpec(a.shape, lambda i: (0,) * a.ndim)
    consts = [mu_wag.reshape(3, 1, d), w0.reshape(1, d), w1, w2, a0.reshape(1, d), a1, a2, g1, g2]
    return pl.pallas_call(
        _rwkv_lora_kernel,
        grid=(t // tm,),
        in_specs=[row, row] + [full(c) for c in consts],
        out_specs=[row, row, row],
        out_shape=[jax.ShapeDtypeStruct((t, d), F32)] * 3,
        compiler_params=_params("parallel"),
        name="rwkv_lora",
    )(h, h_prev, *consts)


def _rwkv_kernel(r_ref, k_ref, v_ref, lw_ref, a_ref, gate_ref, kkw_ref, kaw_ref, rkw_ref,
                 lng_ref, lnb_ref, s0_ref, o_ref, sout_ref, s_ref, *, c_rows, nchunk, npb, ngrp):
    i = pl.program_id(2)
    c2 = 2 * c_rows

    @pl.when(i == 0)
    def _():
        s_ref[...] = s0_ref[...]

    lane = lax.broadcasted_iota(jnp.int32, (c_rows, LANES), 1)
    head_a = lane < RWKV_HEAD
    ri = lax.broadcasted_iota(jnp.int32, (c_rows, c_rows), 0)
    rj = lax.broadcasted_iota(jnp.int32, (c_rows, c_rows), 1)
    tri = (rj <= ri).astype(F32)
    si = lax.broadcasted_iota(jnp.int32, (c2, c2), 0)
    sj = lax.broadcasted_iota(jnp.int32, (c2, c2), 1)
    same_blk = (si // c_rows) == (sj // c_rows)
    strict = jnp.logical_and(same_blk, (sj % c_rows) < (si % c_rows))
    incl = jnp.logical_and(same_blk, (sj % c_rows) <= (si % c_rows))
    inv_n = 1.0 / RWKV_HEAD

    def stack(x):
        return jnp.concatenate([jnp.where(head_a, x, 0.0), jnp.where(head_a, 0.0, x)], axis=0)

    def head_sum(x):
        sa = jnp.sum(jnp.where(head_a, x, 0.0), axis=-1, keepdims=True)
        sb = jnp.sum(jnp.where(head_a, 0.0, x), axis=-1, keepdims=True)
        return jnp.where(head_a, sa, sb)

    def prepare(rows, lanes):
        r, kr, v = r_ref[rows, lanes], k_ref[rows, lanes], v_ref[rows, lanes]
        lw, a = lw_ref[rows, lanes], a_ref[rows, lanes]
        kkw, kaw = kkw_ref[:, lanes], kaw_ref[:, lanes]
        kk = kr * kkw
        ss = head_sum(kk * kk)
        cl = _sel_mm(tri, lw)
        yield
        kk = kk * lax.rsqrt(jnp.maximum(ss, 1e-24))
        k = kr * (1.0 + (a - 1.0) * kaw)
        e_neg = jnp.exp(-cl)
        al = stack(-kk * jnp.exp(cl - lw))
        rt = stack(r * jnp.exp(cl))
        bg = stack(a * kk * e_neg)
        rhs = jnp.concatenate([bg, stack(k * e_neg)], axis=0)
        aa = _mm_nt(jnp.concatenate([al, rt], axis=0), rhs)
        yield
        n_ab = jnp.where(strict, aa[:c2, :c2], 0.0)
        a_ak = jnp.where(strict, aa[:c2, c2:], 0.0)
        a_rb = jnp.where(incl, aa[c2:, :c2], 0.0)
        a_rk = jnp.where(incl, aa[c2:, c2:], 0.0)
        vs = stack(v)
        x = jnp.concatenate([al, _mm(a_ak, vs)], axis=1)
        yield
        pw = n_ab
        x = x + _mm(pw, x)
        yield
        for _ in range(int(math.log2(c_rows)) - 1):
            pw = _mm(pw, pw)
            yield
            x = x + _mm(pw, x)
            yield
        w, u0 = x[:, :LANES], x[:, LANES:]
        y = _mm(a_rb, x)
        yield
        ro = rt + y[:, :LANES]
        o0 = y[:, LANES:] + _mm(a_rk, vs)
        yield
        p = _mm_tn(w, bg)
        yield
        q = _mm_tn(jnp.concatenate([u0, vs], axis=0), rhs)
        yield
        gam = jnp.exp(cl[c_rows - 1:c_rows, :])
        bonus = head_sum(r * k * rkw_ref[:, lanes]) * v
        return ro, o0, p, q, gam, bonus

    def finish(rows, lanes, o2, bonus):
        o = o2[:c_rows] + o2[c_rows:]
        mean = head_sum(o) * inv_n
        dlt = o - mean
        var = head_sum(dlt * dlt) * inv_n
        on =dlt * lax.rsqrt(var + RWKV_GN_EPS) * lng_ref[:, lanes] + lnb_ref[:, lanes]
        o_ref[rows, lanes] = (on + bonus) * gate_ref[rows, lanes]

    def group(g, carry):
        r0 = pl.multiple_of(g * (ngrp * c_rows), ngrp * c_rows)
        lanes = [slice(pi * LANES, (pi + 1) * LANES) for pi in range(npb)]
        rows = [pl.ds(r0 + c * c_rows, c_rows) for c in range(ngrp)]
        units = [(c, pi) for c in range(ngrp) for pi in range(npb)]
        prep = dict(zip(units, _lockstep([prepare(rows[c], lanes[pi]) for c, pi in units])))
        s = [s_ref[pi] for pi in range(npb)]
        for c, pi in units:
            ro, o0, p, q, gam, bonus = prep[c, pi]
            finish(rows[c], lanes[pi], _mm_nt(ro, s[pi]) + o0, bonus)
            s[pi] = (s[pi] + _mm(s[pi], p) + q) * gam
        for pi in range(npb):
            s_ref[pi] = s[pi]
        return carry

    lax.fori_loop(0, nchunk // ngrp, group, 0)

    @pl.when(i == pl.num_programs(2) - 1)
    def _():
        sout_ref[...] = s_ref[...]


def _rwkv_recurrence(rkv, lw, a, gate, kkw, kaw, rkw, lng, lnb, s0_blk, b, t):
    d = lw.shape[1]
    npair = d // LANES
    c_rows = min(CHUNK, t)
    tb = _tile(t, 512)
    nb = t // tb
    nchunk = tb // c_rows
    npb = 2 if npair % 2 == 0 else 1
    ngrp = 2 if nchunk % 2 == 0 else 1
    bw = npb * LANES
    kern = functools.partial(_rwkv_kernel, c_rows=c_rows, nchunk=nchunk, npb=npb, ngrp=ngrp)
    rowblk = pl.BlockSpec((tb, bw), lambda bi, p, i: (bi * nb + i, p))
    rkvblk = lambda c: pl.BlockSpec((None, tb, bw), lambda bi, p, i: (c, bi * nb + i, p))
    vec = pl.BlockSpec((1, bw), lambda bi, p, i: (0, p))
    st = pl.BlockSpec((None, npb, LANES, LANES), lambda bi, p, i: (bi, p, 0, 0))
    return pl.pallas_call(
        kern,
        grid=(b, npair // npb, nb),
        in_specs=[rkvblk(0), rkvblk(1), rkvblk(2), rowblk, rowblk, rowblk, vec, vec, vec, vec, vec, st],
        out_specs=[rowblk, st],
        out_shape=[jax.ShapeDtypeStruct((b * t, d), F32),
                   jax.ShapeDtypeStruct((b, npair, LANES, LANES), F32)],
        scratch_shapes=[pltpu.VMEM((npb, LANES, LANES), F32)],
        compiler_params=_params("parallel", "parallel", "arbitrary"),
        name="rwkv_recurrence",
    )(rkv, rkv, rkv, lw, a, gate, kkw.reshape(1, d), kaw.reshape(1, d), rkw.reshape(1, d),
      lng.reshape(1, d), lnb.reshape(1, d), s0_blk)


def _to_blockdiag(s):
    b, h, n, _ = s.shape
    s = s.reshape(b, h // 2, 2, n, n)
    z = jnp.zeros_like(s[:, :, 0])
    top = jnp.concatenate([s[:, :, 0], z], axis=-1)
    bot = jnp.concatenate([z, s[:, :, 1]], axis=-1)
    return jnp.concatenate([top, bot], axis=-2)


def _from_blockdiag(sb):
    b, p, n2, _ = sb.shape
    n = n2 // 2
    return jnp.stack([sb[:, :, :n, :n], sb[:, :, n:, n:]], axis=2).reshape(b, 2 * p, n, n)


def _trunk(x3, hgrn_s0, k_cache, v_cache, wkv_s0, shift0, w, lower):
    b, t, d = x3.shape
    x = x3.reshape(b * t, d)
    depth = w['norm_g'].shape[0]
    hgrn_out, k_out, v_out, wkv_out, shift_out = [], [], [], [], []
    for layer in range(depth):
        kind, j = layer % 3, layer // 3
        g = w['norm_g'][layer]
        x = _ffn(x, g[0], *w['ffn'][layer][0])
        if kind == 0:
            proj = _norm_mm(x, g[1], w['hgrn_w_in'][j])
            s0_t = jnp.swapaxes(hgrn_s0[j], -1, -2)
            o, s_t = _hgrn_recurrence(proj, lower[layer], s0_t, b, t)
            hgrn_out.append(jnp.swapaxes(s_t, -1, -2))
            x = _gated_proj_res(x, o, proj, 3, w['hgrn_norm_g'][j], w['hgrn_w_out'][j])
        elif kind == 1:
            qkv = _norm_mm(x, g[1], w['attn_w_qkv'][j])
            rel_bias = w['attn_rel_bias'][j]
            nh = rel_bias.shape[0]
            if k_cache is None:
                past = PREV_CHUNKS * CHUNK
                bias = _rel_bias(rel_bias, jnp.arange(CHUNK)[:, None] - jnp.arange(BAND)[None, :] + past)
                o = jnp.concatenate([_attn_prompt(qkv[bi * t:(bi + 1) * t], bias, t) for bi in range(b)], axis=0)
                rows = min(past, t)
                keep = lambda a: a.reshape(b, t, nh, ATTN_HEAD_DIM)[:, t - rows:]
            else:
                rows = k_cache.shape[2]
                bias = _rel_bias(rel_bias, jnp.arange(t)[:, None] + rows - jnp.arange(rows + t)[None, :])
                o = _attn_cached(qkv, k_cache[j].reshape(b, rows, d), v_cache[j].reshape(b, rows, d),
                                 bias[:, :, :rows], bias[:, :, rows:], b, t)
                keep = lambda a: a.reshape(b, t, nh, ATTN_HEAD_DIM)
            k_out.append(keep(qkv[:, d:2 * d]))
            v_out.append(keep(qkv[:, 2 * d:]))
            x = _proj_res(x, o, w['attn_w_out'][j])
        else:
            h = _norm(x, g[1])
            h3 = h.reshape(b, t, d)
            h_prev = jnp.concatenate([shift0[j].astype(F32), h3[:, :-1]], axis=1).reshape(b * t, d)
            mu = w['rwkv_mu'][j]
            rkv = _rwkv_rkv(h, h_prev, mu[:3], w['rwkv_w_rkv'][j])
            lw, a, gate = _rwkv_lora(h, h_prev, mu[3:], w['rwkv_w0'][j], w['rwkv_w1'][j], w['rwkv_w2'][j],
                                     w['rwkv_a0'][j], w['rwkv_a1'][j], w['rwkv_a2'][j],
                                     w['rwkv_g1'][j], w['rwkv_g2'][j])
            o, s_blk = _rwkv_recurrence(rkv, lw, a, gate, w['rwkv_k_k'][j], w['rwkv_k_a'][j],
                                        w['rwkv_r_k'][j].reshape(-1), w['rwkv_ln_g'][j], w['rwkv_ln_b'][j],
                                        _to_blockdiag(wkv_s0[j]), b, t)
            shift_out.append(h3[:, -1:])
            wkv_out.append(_from_blockdiag(s_blk))
            x = _proj_res(x, o, w['rwkv_w_out'][j])
        x = _ffn(x, g[2], *w['ffn'][layer][1])
    y = _norm(x, w['final_norm_g']).reshape(b, t, d)
    return (y, jnp.stack(hgrn_out), jnp.stack(k_out), jnp.stack(v_out),
            jnp.stack(wkv_out), jnp.stack(shift_out))


def kernel(x_prompt, x_sample, state_hgrn, cache_k_band, cache_v_band, state_wkv, state_shift,
           norm_g, final_norm_g, ffn_w_gate_up, ffn_w_down,
           hgrn_w_in, hgrn_lb_logits, hgrn_norm_g, hgrn_w_out,
           attn_w_qkv, attn_rel_bias, attn_w_out,
           rwkv_mu, rwkv_w_rkv, rwkv_w0, rwkv_w1, rwkv_w2, rwkv_a0, rwkv_a1, rwkv_a2,
           rwkv_g1, rwkv_g2, rwkv_k_k, rwkv_k_a, rwkv_r_k, rwkv_ln_g, rwkv_ln_b, rwkv_w_out):
    depth = norm_g.shape[0]
    cast = lambda a: a.astype(MXU_DTYPE)
    ffn = [[_prep_ffn_weights(ffn_w_gate_up[l, i], ffn_w_down[l, i]) for i in range(2)] for l in range(depth)]
    w = dict(norm_g=norm_g, final_norm_g=final_norm_g, ffn=ffn,
             hgrn_w_in=cast(hgrn_w_in), hgrn_norm_g=hgrn_norm_g, hgrn_w_out=cast(hgrn_w_out),
             attn_w_qkv=cast(attn_w_qkv), attn_rel_bias=attn_rel_bias, attn_w_out=cast(attn_w_out),
             rwkv_mu=rwkv_mu, rwkv_w_rkv=cast(rwkv_w_rkv), rwkv_w0=rwkv_w0, rwkv_w1=cast(rwkv_w1),
             rwkv_w2=cast(rwkv_w2), rwkv_a0=rwkv_a0, rwkv_a1=cast(rwkv_a1), rwkv_a2=cast(rwkv_a2),
             rwkv_g1=cast(rwkv_g1), rwkv_g2=cast(rwkv_g2), rwkv_k_k=rwkv_k_k, rwkv_k_a=rwkv_k_a,
             rwkv_r_k=rwkv_r_k, rwkv_ln_g=rwkv_ln_g, rwkv_ln_b=rwkv_ln_b, rwkv_w_out=cast(rwkv_w_out))
    probs = jax.nn.softmax(hgrn_lb_logits.astype(F32), axis=0)
    lower = jnp.cumsum(probs, axis=0) - probs[0]

    b = x_prompt.shape[0]
    d = x_prompt.shape[2]
    n_a, n_c = state_hgrn.shape[0], state_wkv.shape[0]
    hgrn0 = jnp.zeros((n_a, b) + state_hgrn.shape[2:], F32)
    wkv0 = jnp.zeros((n_c, b) + state_wkv.shape[2:], F32)
    shift0 = jnp.zeros((n_c, b, 1, d), F32)
    y_p, hgrn_p, k_p, v_p, wkv_p, shift_p = _trunk(x_prompt, hgrn0, None, None, wkv0, shift0, w, lower)
    y_s, hgrn_s, k_s, v_s, wkv_s, shift_s = _trunk(x_sample, state_hgrn, cache_k_band, cache_v_band,
                                                   state_wkv, state_shift, w, lower)
    return (y_p, y_s, hgrn_p, hgrn_s, k_p, v_p, k_s, v_s, wkv_p, wkv_s, shift_p, shift_s)
```

```python
import functools
import math

import numpy as np
import jax
import jax.numpy as jnp
from jax import lax
from jax.experimental import pallas as pl
from jax.experimental.pallas import tpu as pltpu

F32 = jnp.float32
MXU_DTYPE = jnp.bfloat16

LANES = 128
CHUNK = 64
PREV_CHUNKS = 8
BAND = (PREV_CHUNKS + 1) * CHUNK
REL_CLIP = 128
NORM_EPS = 1e-6
RWKV_GN_EPS = 64e-5
HGRN_HEAD = 128
HGRN_FINE_LEVELS = (8, 4)
ATTN_HEAD_DIM = 64
ATTN_CHUNKS_IN_FLIGHT = 4
RWKV_HEAD = 64
RWKV_CHUNKS_IN_FLIGHT = 4
VMEM_LIMIT =56 * 1024 * 1024


def _params(*sem):
    return pltpu.CompilerParams(dimension_semantics=sem, vmem_limit_bytes=VMEM_LIMIT)


def _tile(n, pref):
    if n <= pref:
        return n
    t = pref
    while n % t:
        t //= 2
    return t


def _mm(a, b):
    return jnp.dot(a.astype(MXU_DTYPE), b.astype(MXU_DTYPE), preferred_element_type=F32)


def _mm_nt(a, b):
    return lax.dot_general(a.astype(MXU_DTYPE), b.astype(MXU_DTYPE),
                           (((1,), (1,)), ((), ())), preferred_element_type=F32)


def _mm_tn(a, b):
    return lax.dot_general(a.astype(MXU_DTYPE), b.astype(MXU_DTYPE),
                           (((0,), (0,)), ((), ())), preferred_element_type=F32)


def _split3(x):
    hi = x.astype(jnp.bfloat16)
    r1 = x - hi.astype(F32)
    mid = r1.astype(jnp.bfloat16)
    lo = (r1 - mid.astype(F32)).astype(jnp.bfloat16)
    return hi, mid, lo


def _sel_mm(sel, x):
    sel = sel.astype(jnp.bfloat16)
    hi, mid, lo = _split3(x)
    d = lambda p: jnp.dot(sel, p, preferred_element_type=F32)
    return d(hi) + (d(mid) + d(lo))


def _lockstep(gens):
    out = [None] * len(gens)
    live = list(range(len(gens)))
    while live:
        for n in list(live):
            try:
                next(gens[n])
            except StopIteration as stop:
                out[n] = stop.value
                live.remove(n)
    return out


def _rms(x):
    return x * lax.rsqrt(jnp.mean(x * x, axis=-1, keepdims=True) + NORM_EPS)


def _sigmoid(x):
    return 1.0 / (1.0 + jnp.exp(-x))


def _silu(x):
    return x * _sigmoid(x)


def _ffn_up_kernel(x_ref, g_ref, wg_ref, wu_ref, h_ref, xn_ref):
    @pl.when(pl.program_id(1) == 0)
    def _():
        xn_ref[...] = (_rms(x_ref[...]) * g_ref[...]).astype(xn_ref.dtype)

    xn = xn_ref[...]
    gate = jnp.dot(xn, wg_ref[...], preferred_element_type=F32)
    up = jnp.dot(xn, wu_ref[...], preferred_element_type=F32)
    h_ref[...] = (_silu(gate) * up).astype(h_ref.dtype)


def _ffn(x, g, w_gu, w_d, tf):
    t, d = x.shape
    fp = w_d.shape[0]
    nf = fp // tf
    tm = _tile(t, 1024)
    h = pl.pallas_call(
        _ffn_up_kernel,
        grid=(t // tm, nf),
        in_specs=[
            pl.BlockSpec((tm, d), lambda i, j: (i, 0)),
            pl.BlockSpec((1, d), lambda i, j: (0, 0)),
            pl.BlockSpec((d, tf), lambda i, j: (0, j)),
            pl.BlockSpec((d, tf), lambda i, j: (0, nf + j)),
        ],
        out_specs=pl.BlockSpec((tm, tf), lambda i, j: (i, j)),
        out_shape=jax.ShapeDtypeStruct((t, fp), MXU_DTYPE),
        scratch_shapes=[pltpu.VMEM((tm, d), MXU_DTYPE)],
        compiler_params=_params("parallel", "arbitrary"),
        name="ffn_up",
    )(x, g.reshape(1, d), w_gu, w_gu)
    return _proj_res(x, h, w_d, scale=0.5)


def _prep_ffn_weights(w_gate_up, w_down):
    d, f2 = w_gate_up.shape
    f = f2 // 2
    tf = 512 if f > 512 else f
    fp = -(-f // tf) * tf
    pad = fp - f
    w_gu = jnp.pad(w_gate_up.astype(MXU_DTYPE).reshape(d, 2, f), ((0, 0), (0, 0), (0, pad))).reshape(d, 2 * fp)
    w_d = jnp.pad(w_down.astype(MXU_DTYPE), ((0, pad), (0, 0)))
    return w_gu, w_d, tf


def _norm_mm_kernel(x_ref, g_ref, w_ref, o_ref, xn_ref):
    @pl.when(pl.program_id(1) == 0)
    def _():
        xn_ref[...] = (_rms(x_ref[...]) * g_ref[...]).astype(xn_ref.dtype)

    o_ref[...] = jnp.dot(xn_ref[...], w_ref[...], preferred_element_type=F32)


def _norm_mm(x, g, w):
    t, d = x.shape
    n = w.shape[1]
    tm, tn = _tile(t, 1024), _tile(n, 1024)
    return pl.pallas_call(
        _norm_mm_kernel,
        grid=(t // tm, n // tn),
        in_specs=[
            pl.BlockSpec((tm, d), lambda i, j: (i, 0)),
            pl.BlockSpec((1, d), lambda i, j: (0, 0)),
            pl.BlockSpec((d, tn), lambda i, j: (0, j)),
        ],
        out_specs=pl.BlockSpec((tm, tn), lambda i, j: (i, j)),
        out_shape=jax.ShapeDtypeStruct((t, n), F32),
        scratch_shapes=[pltpu.VMEM((tm, d), MXU_DTYPE)],
        compiler_params=_params("parallel", "arbitrary"),
        name="norm_mm",
    )(x, g.reshape(1, d), w)


def _norm_kernel(x_ref, g_ref, o_ref):
    o_ref[...] = _rms(x_ref[...]) * g_ref[...]


def _norm(x, g):
    t, d = x.shape
    tm = _tile(t, 512)
    return pl.pallas_call(
        _norm_kernel,
        grid=(t // tm,),
        in_specs=[pl.BlockSpec((tm, d), lambda i: (i, 0)), pl.BlockSpec((1, d), lambda i: (0, 0))],
        out_specs=pl.BlockSpec((tm, d), lambda i: (i, 0)),
        out_shape=jax.ShapeDtypeStruct((t, d), F32),
        compiler_params=_params("parallel"),
        name="norm",
    )(x, g.reshape(1, d))


def _proj_res_kernel(x_ref, a_ref, w_ref, o_ref, *, scale):
    y = _mm(a_ref[...], w_ref[...])
    o_ref[...] = x_ref[...] + (y if scale == 1.0 else scale * y)


def _gated_proj_res_kernel(x_ref, a_ref, z_ref, g_ref, w_ref, o_ref, *, sub):
    for r in range(0, x_ref.shape[0], sub):
        rows = slice(r, r + sub)
        an = _rms(a_ref[rows, :]) * g_ref[...] * _silu(z_ref[rows, :])
        o_ref[rows, :] = x_ref[rows, :] + _mm(an, w_ref[...])


def _proj_res(x, a, w, scale=1.0):
    t, d = x.shape
    k = a.shape[1]
    tm, tn = _tile(t, 512), _tile(d, 1024)
    return pl.pallas_call(
        functools.partial(_proj_res_kernel, scale=scale),
        grid=(t // tm, d // tn),
        in_specs=[
            pl.BlockSpec((tm, tn), lambda i, j: (i, j)),
            pl.BlockSpec((tm, k), lambda i, j: (i, 0)),
            pl.BlockSpec((k, tn), lambda i, j: (0, j)),
        ],
        out_specs=pl.BlockSpec((tm, tn), lambda i, j: (i, j)),
        out_shape=jax.ShapeDtypeStruct((t, d), F32),
        compiler_params=_params("parallel", "arbitrary"),
        name="proj_res",
    )(x, a, w)


def _gated_proj_res(x, a, z_src, z_col, g, w):
    t, d = x.shape
    tm = _tile(t, 512)
    row = lambda col: pl.BlockSpec((tm, d), lambda i: (i, col))
    return pl.pallas_call(
        functools.partial(_gated_proj_res_kernel, sub=_tile(tm, 128)),
        grid=(t // tm,),
        in_specs=[row(0), row(0), row(z_col),
                  pl.BlockSpec((1, d), lambda i: (0, 0)),
                  pl.BlockSpec((d, d), lambda i: (0, 0))],
        out_specs=row(0),
        out_shape=jax.ShapeDtypeStruct((t, d), F32),
        compiler_params=_params("parallel"),
        name="gated_proj_res",
    )(x, a, z_src, g.reshape(1, d), w)


def _hgrn_levels(rows):
    t = np.arange(rows)[:, None]
    r = np.arange(rows)[None, :]
    mats = [r <= t]
    for b in HGRN_FINE_LEVELS:
        mid = (t // b) * b + b // 2 - 1
        upper = (t % b) >= b // 2
        mats.append(np.where(upper, (r > mid) & (r <= t), (r > t) & (r <= mid)))
    return np.concatenate(mats, axis=0).astype(np.float32)


def _hgrn_kernel(q_ref, f_ref, v_ref, lb_ref, lvl_ref, s0_ref, o_ref, sout_ref, st_ref, *, rows, nblk):
    i = pl.program_id(2)

    @pl.when(i == 0)
    def _():
        st_ref[...] = s0_ref[0, 0]

    lb = lb_ref[...]
    lvl = lvl_ref[...]
    t_id = lax.broadcasted_iota(jnp.int32, (rows, HGRN_HEAD), 0)
    ti = lax.broadcasted_iota(jnp.int32, (rows, rows), 0)
    si = lax.broadcasted_iota(jnp.int32, (rows, rows), 1)
    scale = HGRN_HEAD ** -0.5

    def level(scores, b, qe, ke):
        upper = (t_id & (b - 1)) >= b // 2
        s_l = _mm_nt(jnp.where(upper, qe, 0.0), jnp.where(upper, 0.0, ke))
        if b < rows:
            sh = int(math.log2(b))
            s_l = jnp.where((ti >> sh) == (si >> sh), s_l, 0.0)
        return scores + s_l

    def block(sl):
        q = _silu(q_ref[sl, :]) * scale
        fz = f_ref[sl, :]
        v = v_ref[sl, :]
        f = lb + (1.0 - lb) * _sigmoid(fz)
        k = (1.0 - lb) * _sigmoid(-fz)
        sums = _sel_mm(lvl, jnp.log(f))
        yield
        g = sums[:rows]
        scores = jnp.where(ti == si, _mm_nt(q, k), 0.0)
        yield
        scores = level(scores, 2, q * f, k)
        yield
        for n, b in enumerate(HGRN_FINE_LEVELS):
            if b <= rows:
                e = jnp.exp(sums[(n + 1) * rows:(n + 2) * rows])
                scores = level(scores, b, q * e, k * e)
                yield
        b = 2 * HGRN_FINE_LEVELS[0]
        while b <= rows:
            g_mid = jnp.concatenate(
                [jnp.broadcast_to(g[m:m + 1, :], (b, HGRN_HEAD)) for m in range(b // 2 - 1, rows, b)], axis=0)
            upper = (t_id & (b - 1)) >= b // 2
            e = jnp.exp(jnp.where(upper, g - g_mid, g_mid - g))
            scores = level(scores, b, q * e, k * e)
            yield
            b *= 2
        g_last = g[rows - 1:rows, :]
        o_intra = _mm(scores, v)
        yield
        kv = _mm_tn(v, k * jnp.exp(g_last - g))
        return o_intra, q * jnp.exp(g), jnp.exp(g_last), kv

    slices = [slice(u * rows, (u + 1) * rows) for u in range(nblk)]
    st = st_ref[...]
    blocks = _lockstep([block(sl) for sl in slices])
    for sl, (o_intra, qg, dec, kv) in zip(slices, blocks):
        o_ref[sl, :] = o_intra + _mm_nt(qg, st)
        st = st * dec + kv
    st_ref[...] = st

    @pl.when(i == pl.num_programs(2) - 1)
    def _():
        sout_ref[0, 0] = st


def _hgrn_recurrence(proj, lb, s0_t, b, t):
    d = proj.shape[1] // 4
    h = d // HGRN_HEAD
    tb = _tile(t, 512)
    nb = t // tb
    rows = _tile(tb, 128)
    lvl = jnp.asarray(_hgrn_levels(rows), dtype=jnp.bfloat16)
    kern = functools.partial(_hgrn_kernel, rows=rows, nblk=tb // rows)
    row = lambda bi, hi, i: bi * nb + i
    return pl.pallas_call(
        kern,
        grid=(b, h, nb),
        in_specs=[
            pl.BlockSpec((tb, HGRN_HEAD), lambda bi, hi, i: (row(bi, hi, i), hi)),
            pl.BlockSpec((tb, HGRN_HEAD), lambda bi, hi, i: (row(bi, hi, i), h + hi)),
            pl.BlockSpec((tb, HGRN_HEAD), lambda bi, hi, i: (row(bi, hi, i), 2 * h + hi)),
            pl.BlockSpec((1, HGRN_HEAD), lambda bi, hi, i: (0, hi)),
            pl.BlockSpec(lvl.shape, lambda bi, hi, i: (0, 0)),
            pl.BlockSpec((1, 1, HGRN_HEAD, HGRN_HEAD), lambda bi, hi, i: (bi, hi, 0, 0)),
        ],
        out_specs=[
            pl.BlockSpec((tb, HGRN_HEAD), lambda bi, hi, i: (row(bi, hi, i), hi)),
            pl.BlockSpec((1, 1, HGRN_HEAD, HGRN_HEAD), lambda bi, hi, i: (bi, hi, 0, 0)),
        ],
        out_shape=[jax.ShapeDtypeStruct((b * t, d), F32),
                   jax.ShapeDtypeStruct((b, h, HGRN_HEAD, HGRN_HEAD), F32)],
        scratch_shapes=[pltpu.VMEM((HGRN_HEAD, HGRN_HEAD), F32)],
        compiler_params=_params("parallel", "parallel", "arbitrary"),
        name="hgrn_recurrence",
    )(proj, proj, proj, lb.reshape(1, d), lvl, s0_t)


def _softmax_pv(parts):
    m = functools.reduce(jnp.maximum, [jnp.max(s, axis=-1, keepdims=True) for s, _ in parts])
    es = [jnp.exp(s - m) for s, _ in parts]
    den = functools.reduce(jnp.add, [jnp.sum(e, axis=-1, keepdims=True) for e in es])
    yield
    pv = functools.reduce(jnp.add, [_mm(e, v) for e, (_, v) in zip(es, parts)])
    yield
    return pv / den


def _attn_prompt_kernel(q_ref, kp_ref, kc_ref, vp_ref, vc_ref, bias_ref, o_ref, k_scr, v_scr, *, tq):
    i = pl.program_id(2)
    k_scr[0:tq, :] = kp_ref[...].astype(k_scr.dtype)
    k_scr[tq:2 * tq, :] = kc_ref[...].astype(k_scr.dtype)
    v_scr[0:tq, :] = vp_ref[...].astype(v_scr.dtype)
    v_scr[tq:2 * tq, :] = vc_ref[...].astype(v_scr.dtype)
    lane = lax.broadcasted_iota(jnp.int32, (CHUNK, LANES), 1)
    head_a = lane < ATTN_HEAD_DIM
    col = lax.broadcasted_iota(jnp.int32, (CHUNK, BAND), 1)
    scale = ATTN_HEAD_DIM ** -0.5
    past = PREV_CHUNKS * CHUNK

    def unit(j, hd, masked):
        q = q_ref[j * CHUNK:(j + 1) * CHUNK, :] * scale
        w0 = tq - past + j * CHUNK
        sel = head_a if hd == 0 else jnp.logical_not(head_a)
        s = _mm_nt(jnp.where(sel, q, 0.0), k_scr[w0:w0 + BAND, :]) + bias_ref[hd]
        yield
        if masked:
            s = jnp.where(col + ((i - 1) * tq + w0) >= 0, s, -jnp.inf)
        return (yield from _softmax_pv([(s, v_scr[w0:w0 + BAND, :])]))

    def run(masked):
        for j0 in range(0, tq // CHUNK, ATTN_CHUNKS_IN_FLIGHT):
            js = range(j0, min(j0 + ATTN_CHUNKS_IN_FLIGHT, tq // CHUNK))
            outs = _lockstep([unit(j, hd, masked) for j in js for hd in range(2)])
            for n, j in enumerate(js):
                o_ref[j * CHUNK:(j + 1) * CHUNK, :] = jnp.where(head_a, outs[2 * n], outs[2 * n + 1])

    @pl.when(i == 0)
    def _():
        run(True)

    @pl.when(i > 0)
    def _():
        run(False)


def _attn_prompt(qkv, bias, b, t):
    d = qkv.shape[1] // 3
    npair = d // LANES
    tq = _tile(t, 512)
    nq = t // tq
    assert tq >= PREV_CHUNKS * CHUNK and tq % CHUNK == 0
    cur = lambda bi, i: bi * nq + i
    prev = lambda bi, i: bi * nq + jnp.maximum(i - 1, 0)
    return pl.pallas_call(
        functools.partial(_attn_prompt_kernel, tq=tq),
        grid=(b, npair, nq),
        in_specs=[
            pl.BlockSpec((tq, LANES), lambda bi, p, i: (cur(bi, i), p)),
            pl.BlockSpec((tq, LANES), lambda bi, p, i: (prev(bi, i), npair + p)),
            pl.BlockSpec((tq, LANES), lambda bi, p, i: (cur(bi, i), npair + p)),
            pl.BlockSpec((tq, LANES), lambda bi, p, i: (prev(bi, i), 2 * npair + p)),
            pl.BlockSpec((tq, LANES), lambda bi, p, i: (cur(bi, i), 2 * npair + p)),
            pl.BlockSpec((2, CHUNK, BAND), lambda bi, p, i: (p, 0, 0)),
        ],
        out_specs=pl.BlockSpec((tq, LANES), lambda bi, p, i: (cur(bi, i), p)),
        out_shape=jax.ShapeDtypeStruct((b * t, d), F32),
        scratch_shapes=[pltpu.VMEM((2 * tq, LANES), MXU_DTYPE), pltpu.VMEM((2 * tq, LANES), MXU_DTYPE)],
        compiler_params=_params("parallel", "parallel", "arbitrary"),
        name="attn_prompt",
    )(qkv, qkv, qkv, qkv, qkv, bias)


def _attn_cached_kernel(q_ref, kn_ref, vn_ref, kc_ref, vc_ref, bc_ref, bn_ref, o_ref):
    q = q_ref[...]
    lane = lax.broadcasted_iota(jnp.int32, q.shape, 1)
    head_a = lane < ATTN_HEAD_DIM
    scale = ATTN_HEAD_DIM ** -0.5
    kc, vc, kn, vn = kc_ref[0], vc_ref[0], kn_ref[...], vn_ref[...]

    def unit(hd, sel):
        qm = jnp.where(sel, q * scale, 0.0)
        s_c = _mm_nt(qm, kc) + bc_ref[hd]
        s_n = _mm_nt(qm, kn) + bn_ref[hd]
        return (yield from _softmax_pv([(s_c, vc), (s_n, vn)]))

    outs = _lockstep([unit(0, head_a), unit(1, jnp.logical_not(head_a))])
    o_ref[...] = jnp.where(head_a, outs[0], outs[1])


def _attn_cached(qkv, k_cache, v_cache, bias_c, bias_n, b, t):
    d = qkv.shape[1] // 3
    npair = d // LANES
    rows = k_cache.shape[1]
    return pl.pallas_call(
        _attn_cached_kernel,
        grid=(b, npair),
        in_specs=[
            pl.BlockSpec((t, LANES), lambda bi, p: (bi, p)),
            pl.BlockSpec((t, LANES), lambda bi, p: (bi, npair + p)),
            pl.BlockSpec((t, LANES), lambda bi, p: (bi, 2 * npair + p)),
            pl.BlockSpec((1, rows, LANES), lambda bi, p: (bi, 0, p)),
            pl.BlockSpec((1, rows, LANES), lambda bi, p: (bi, 0, p)),
            pl.BlockSpec((2, t, rows), lambda bi, p: (p, 0, 0)),
            pl.BlockSpec((2, t, t), lambda bi, p: (p, 0, 0)),
        ],
        out_specs=pl.BlockSpec((t, LANES), lambda bi, p: (bi, p)),
        out_shape=jax.ShapeDtypeStruct((b * t, d), F32),
        compiler_params=_params("parallel", "parallel"),
        name="attn_cached",
    )(qkv, qkv, qkv, k_cache, v_cache, bias_c, bias_n)


def _rel_bias(rel_bias, nq, nk, offset):
    rel = jnp.arange(nq + nk - 1) + (offset - nk + 1)
    diag = rel_bias[:, jnp.clip(rel, -REL_CLIP, REL_CLIP) + REL_CLIP].astype(F32)
    rev = diag[:, ::-1]
    return jnp.stack([rev[:, nq - 1 - q:nq - 1 - q + nk] for q in range(nq)], axis=1)


def _rwkv_rkv_kernel(h_ref, p_ref, mu_ref, w_ref, o_ref, l_ref):
    @pl.when(pl.program_id(2) == 0)
    def _():
        h = h_ref[...]
        l_ref[...] = (h + (p_ref[...] - h) * mu_ref[0]).astype(l_ref.dtype)

    o_ref[0] = jnp.dot(l_ref[...], w_ref[0], preferred_element_type=F32)


def _rwkv_rkv(h, h_prev, mu, w_rkv):
    t, d = h.shape
    tm, tn = _tile(t, 512), _tile(d, 2048)
    return pl.pallas_call(
        _rwkv_rkv_kernel,
        grid=(3, t // tm, d // tn),
        in_specs=[
            pl.BlockSpec((tm, d), lambda c, i, j: (i, 0)),
            pl.BlockSpec((tm, d), lambda c, i, j: (i, 0)),
            pl.BlockSpec((1, 1, d), lambda c, i, j: (c, 0, 0)),
            pl.BlockSpec((1, d, tn), lambda c, i, j: (c, 0, j)),
        ],
        out_specs=pl.BlockSpec((1, tm, tn), lambda c, i, j: (c, i, j)),
        out_shape=jax.ShapeDtypeStruct((3, t, d), F32),
        scratch_shapes=[pltpu.VMEM((tm, d), MXU_DTYPE)],
        compiler_params=_params("parallel", "parallel", "arbitrary"),
        name="rwkv_rkv",
    )(h, h_prev, mu.reshape(-1, 1, d), w_rkv)


def _softplus(y):
    return jnp.maximum(y, 0.0) + jnp.log1p(jnp.exp(-jnp.abs(y)))


def _rwkv_lora_kernel(h_ref, p_ref, mu_ref, w0_ref, w1_ref, w2_ref, a0_ref, a1_ref, a2_ref,
                      g1_ref, g2_ref, lw_ref, a_ref, gate_ref):
    h = h_ref[...]
    xx = p_ref[...] - h
    lerp = lambda c: h + xx * mu_ref[c]
    z = w0_ref[...] + _mm(jnp.tanh(_mm(lerp(0), w1_ref[...])), w2_ref[...])
    w_log = -_softplus(-z) - 0.5
    lw_ref[...] = -jnp.exp(w_log)
    a_ref[...] = _sigmoid(a0_ref[...] + _mm(_mm(lerp(1), a1_ref[...]), a2_ref[...]))
    gate_ref[...] = _mm(_sigmoid(_mm(lerp(2), g1_ref[...])), g2_ref[...])


def _rwkv_lora(h, h_prev, mu_wag, w0, w1, w2, a0, a1, a2, g1, g2):
    t, d = h.shape
    tm = _tile(t, 256)
    row = pl.BlockSpec((tm, d), lambda i: (i, 0))
    full = lambda a: pl.BlockSpec(a.shape, lambda i: (0,) * a.ndim)
    consts = [mu_wag.reshape(3, 1, d), w0.reshape(1, d), w1, w2, a0.reshape(1, d), a1, a2, g1, g2]
    return pl.pallas_call(
        _rwkv_lora_kernel,
        grid=(t // tm,),
        in_specs=[row, row] + [full(c) for c in consts],
        out_specs=[row, row, row],
        out_shape=[jax.ShapeDtypeStruct((t, d), F32)] * 3,
        compiler_params=_params("parallel"),
        name="rwkv_lora",
    )(h, h_prev, *consts)


def _rwkv_kernel(r_ref, k_ref, v_ref, lw_ref, a_ref, gate_ref, kkw_ref, kaw_ref, rkw_ref,
                 lng_ref, lnb_ref, s0_ref, o_ref, sout_ref, s_ref, *, c_rows, nchunk, npb, ngrp):
    i = pl.program_id(2)
    c2 = 2 * c_rows

    @pl.when(i == 0)
    def _():
        s_ref[...] = s0_ref[...]

    lane = lax.broadcasted_iota(jnp.int32, (c_rows, LANES), 1)
    head_a = lane < RWKV_HEAD
    ri = lax.broadcasted_iota(jnp.int32, (c_rows, c_rows), 0)
    rj = lax.broadcasted_iota(jnp.int32, (c_rows, c_rows), 1)
    tri = (rj <= ri).astype(F32)
    si = lax.broadcasted_iota(jnp.int32, (c2, c2), 0)
    sj = lax.broadcasted_iota(jnp.int32, (c2, c2), 1)
    same_blk = (si // c_rows) == (sj // c_rows)
    strict = jnp.logical_and(same_blk, (sj % c_rows) < (si % c_rows))
    incl = jnp.logical_and(same_blk, (sj % c_rows) <= (si % c_rows))
    inv_n = 1.0 / RWKV_HEAD

    def stack(x):
        return jnp.concatenate([jnp.where(head_a, x, 0.0), jnp.where(head_a, 0.0, x)], axis=0)

    def head_sum(x):
        sa = jnp.sum(jnp.where(head_a, x, 0.0), axis=-1, keepdims=True)
        sb = jnp.sum(jnp.where(head_a, 0.0, x), axis=-1, keepdims=True)
        return jnp.where(head_a, sa, sb)

    def prepare(rows, lanes):
        r, kr, v = r_ref[rows, lanes], k_ref[rows, lanes], v_ref[rows, lanes]
        lw, a = lw_ref[rows, lanes], a_ref[rows, lanes]
        kkw, kaw = kkw_ref[:, lanes], kaw_ref[:, lanes]
        kk = kr * kkw
        ss = head_sum(kk * kk)
        cl = _sel_mm(tri, lw)
        yield
        kk = kk * lax.rsqrt(jnp.maximum(ss, 1e-24))
        k = kr * (1.0 + (a - 1.0) * kaw)
        e_neg = jnp.exp(-cl)
        al = stack(-kk * jnp.exp(cl - lw))
        rt = stack(r * jnp.exp(cl))
        bg = stack(a * kk * e_neg)
        rhs = jnp.concatenate([bg, stack(k * e_neg)], axis=0)
        aa = _mm_nt(jnp.concatenate([al, rt], axis=0), rhs)
        yield
        n_ab = jnp.where(strict, aa[:c2, :c2], 0.0)
        a_ak = jnp.where(strict, aa[:c2, c2:], 0.0)
        a_rb = jnp.where(incl, aa[c2:, :c2], 0.0)
        a_rk = jnp.where(incl, aa[c2:, c2:], 0.0)
        vs = stack(v)
        x = jnp.concatenate([al, _mm(a_ak, vs)], axis=1)
        yield
        pw = n_ab
        x = x + _mm(pw, x)
        yield
        for _ in range(int(math.log2(c_rows)) - 1):
            pw = _mm(pw, pw)
            yield
            x = x + _mm(pw, x)
            yield
        w, u0 = x[:, :LANES], x[:, LANES:]
        y = _mm(a_rb, x)
        yield
        ro = rt + y[:, :LANES]
        o0 = y[:, LANES:] + _mm(a_rk, vs)
        yield
        p = _mm_tn(w, bg)
        yield
        q = _mm_tn(jnp.concatenate([u0, vs], axis=0), rhs)
        yield
        gam = jnp.exp(cl[c_rows - 1:c_rows, :])
        bonus = head_sum(r * k * rkw_ref[:, lanes]) * v
        return ro, o0, p, q, gam, bonus

    def finish(rows, lanes, o2, bonus):
        o = o2[:c_rows] + o2[c_rows:]
        mean = head_sum(o) * inv_n
        dlt = o - mean
        var = head_sum(dlt * dlt) * inv_n
        on =dlt * lax.rsqrt(var + RWKV_GN_EPS) * lng_ref[:, lanes] + lnb_ref[:, lanes]
        o_ref[rows, lanes] = (on + bonus) * gate_ref[rows, lanes]

    def group(g, carry):
        r0 = pl.multiple_of(g * (ngrp * c_rows), ngrp * c_rows)
        lanes = [slice(pi * LANES, (pi + 1) * LANES) for pi in range(npb)]
        rows = [pl.ds(r0 + c * c_rows, c_rows) for c in range(ngrp)]
        units = [(c, pi) for c in range(ngrp) for pi in range(npb)]
        prep = dict(zip(units, _lockstep([prepare(rows[c], lanes[pi]) for c, pi in units])))
        s = [s_ref[pi] for pi in range(npb)]
        for c, pi in units:
            ro, o0, p, q, gam, bonus = prep[c, pi]
            finish(rows[c], lanes[pi], _mm_nt(ro, s[pi]) + o0, bonus)
            s[pi] = (s[pi] + _mm(s[pi], p) + q) * gam
        for pi in range(npb):
            s_ref[pi] = s[pi]
        return carry

    lax.fori_loop(0, nchunk // ngrp, group, 0)

    @pl.when(i == pl.num_programs(2) - 1)
    def _():
        sout_ref[...] = s_ref[...]


def _rwkv_recurrence(rkv, lw, a, gate, kkw, kaw, rkw, lng, lnb, s0_blk, b, t):
    d = lw.shape[1]
    npair = d // LANES
    c_rows = min(CHUNK, t)
    tb = _tile(t, 512)
    nb = t // tb
    nchunk = tb // c_rows
    npb = 2 if npair % 2 == 0 else 1
    ngrp = _tile(nchunk, RWKV_CHUNKS_IN_FLIGHT)
    bw = npb * LANES
    kern = functools.partial(_rwkv_kernel, c_rows=c_rows, nchunk=nchunk, npb=npb, ngrp=ngrp)
    rowblk = pl.BlockSpec((tb, bw), lambda bi, p, i: (bi * nb + i, p))
    rkvblk = lambda c: pl.BlockSpec((None, tb, bw), lambda bi, p, i: (c, bi * nb + i, p))
    vec = pl.BlockSpec((1, bw), lambda bi, p, i: (0, p))
    st = pl.BlockSpec((None, npb, LANES, LANES), lambda bi, p, i: (bi, p, 0, 0))
    return pl.pallas_call(
        kern,
        grid=(b, npair // npb, nb),
        in_specs=[rkvblk(0), rkvblk(1), rkvblk(2), rowblk, rowblk, rowblk, vec, vec, vec, vec, vec, st],
        out_specs=[rowblk, st],
        out_shape=[jax.ShapeDtypeStruct((b * t, d), F32),
                   jax.ShapeDtypeStruct((b, npair, LANES, LANES), F32)],
        scratch_shapes=[pltpu.VMEM((npb, LANES, LANES), F32)],
        compiler_params=_params("parallel", "parallel", "arbitrary"),
        name="rwkv_recurrence",
    )(rkv, rkv, rkv, lw, a, gate, kkw.reshape(1, d), kaw.reshape(1, d), rkw.reshape(1, d),
      lng.reshape(1, d), lnb.reshape(1, d), s0_blk)


def _to_blockdiag(s):
    b, h, n, _ = s.shape
    s = s.reshape(b, h // 2, 2, n, n)
    z = jnp.zeros_like(s[:, :, 0])
    top = jnp.concatenate([s[:, :, 0], z], axis=-1)
    bot = jnp.concatenate([z, s[:, :, 1]], axis=-1)
    return jnp.concatenate([top, bot], axis=-2)


def _from_blockdiag(sb):
    b, p, n2, _ = sb.shape
    n = n2 // 2
    return jnp.stack([sb[:, :, :n, :n], sb[:, :, n:, n:]], axis=2).reshape(b, 2 * p, n, n)


def _trunk(x3, hgrn_s0, k_cache, v_cache, wkv_s0, shift0, w, lower):
    b, t, d = x3.shape
    x = x3.reshape(b * t, d)
    depth = w['norm_g'].shape[0]
    hgrn_out, k_out, v_out, wkv_out, shift_out = [], [], [], [], []
    for layer in range(depth):
        kind, j = layer % 3, layer // 3
        g = w['norm_g'][layer]
        x = _ffn(x, g[0], *w['ffn'][layer][0])
        if kind == 0:
            proj = _norm_mm(x, g[1], w['hgrn_w_in'][j])
            s0_t = jnp.swapaxes(hgrn_s0[j], -1, -2)
            o, s_t = _hgrn_recurrence(proj, lower[layer], s0_t, b, t)
            hgrn_out.append(jnp.swapaxes(s_t, -1, -2))
            x = _gated_proj_res(x, o, proj, 3, w['hgrn_norm_g'][j], w['hgrn_w_out'][j])
        elif kind == 1:
            qkv = _norm_mm(x, g[1], w['attn_w_qkv'][j])
            rel_bias = w['attn_rel_bias'][j]
            nh = rel_bias.shape[0]
            if k_cache is None:
                past = PREV_CHUNKS * CHUNK
                o = _attn_prompt(qkv, _rel_bias(rel_bias, CHUNK, BAND, past), b, t)
                rows = min(past, t)
                keep = lambda a: a.reshape(b, t, nh, ATTN_HEAD_DIM)[:, t - rows:]
            else:
                rows = k_cache.shape[2]
                bias = _rel_bias(rel_bias, t, rows + t, rows)
                o =_attn_cached(qkv, k_cache[j].reshape(b, rows, d), v_cache[j].reshape(b, rows, d),
                                 bias[:, :, :rows], bias[:, :, rows:], b, t)
                keep = lambda a: a.reshape(b, t, nh, ATTN_HEAD_DIM)
            k_out.append(keep(qkv[:, d:2 * d]))
            v_out.append(keep(qkv[:, 2 * d:]))
            x = _proj_res(x, o, w['attn_w_out'][j])
        else:
            h = _norm(x, g[1])
            h3 = h.reshape(b, t, d)
            h_prev = jnp.concatenate([shift0[j].astype(F32), h3[:, :-1]], axis=1).reshape(b * t, d)
            mu = w['rwkv_mu'][j]
            rkv = _rwkv_rkv(h, h_prev, mu[:3], w['rwkv_w_rkv'][j])
            lw, a, gate = _rwkv_lora(h, h_prev, mu[3:], w['rwkv_w0'][j], w['rwkv_w1'][j], w['rwkv_w2'][j],
                                     w['rwkv_a0'][j], w['rwkv_a1'][j], w['rwkv_a2'][j],
                                     w['rwkv_g1'][j], w['rwkv_g2'][j])
            o, s_blk = _rwkv_recurrence(rkv, lw, a, gate, w['rwkv_k_k'][j], w['rwkv_k_a'][j],
                                        w['rwkv_r_k'][j].reshape(-1), w['rwkv_ln_g'][j], w['rwkv_ln_b'][j],
                                        _to_blockdiag(wkv_s0[j]), b, t)
            shift_out.append(h3[:, -1:])
            wkv_out.append(_from_blockdiag(s_blk))
            x = _proj_res(x, o, w['rwkv_w_out'][j])
        x = _ffn(x, g[2], *w['ffn'][layer][1])
    y = _norm(x, w['final_norm_g']).reshape(b, t, d)
    return (y, jnp.stack(hgrn_out), jnp.stack(k_out), jnp.stack(v_out),
            jnp.stack(wkv_out), jnp.stack(shift_out))


def kernel(x_prompt, x_sample, state_hgrn, cache_k_band, cache_v_band, state_wkv, state_shift,
           norm_g, final_norm_g, ffn_w_gate_up, ffn_w_down,
           hgrn_w_in, hgrn_lb_logits, hgrn_norm_g, hgrn_w_out,
           attn_w_qkv, attn_rel_bias, attn_w_out,
           rwkv_mu, rwkv_w_rkv, rwkv_w0, rwkv_w1, rwkv_w2, rwkv_a0, rwkv_a1, rwkv_a2,
           rwkv_g1, rwkv_g2, rwkv_k_k, rwkv_k_a, rwkv_r_k, rwkv_ln_g, rwkv_ln_b, rwkv_w_out):
    depth = norm_g.shape[0]
    cast = lambda a: a.astype(MXU_DTYPE)
    ffn = [[_prep_ffn_weights(ffn_w_gate_up[l, i], ffn_w_down[l, i]) for i in range(2)] for l in range(depth)]
    w = dict(norm_g=norm_g, final_norm_g=final_norm_g, ffn=ffn,
             hgrn_w_in=cast(hgrn_w_in), hgrn_norm_g=hgrn_norm_g, hgrn_w_out=cast(hgrn_w_out),
             attn_w_qkv=cast(attn_w_qkv), attn_rel_bias=attn_rel_bias, attn_w_out=cast(attn_w_out),
             rwkv_mu=rwkv_mu, rwkv_w_rkv=cast(rwkv_w_rkv), rwkv_w0=rwkv_w0, rwkv_w1=cast(rwkv_w1),
             rwkv_w2=cast(rwkv_w2), rwkv_a0=rwkv_a0, rwkv_a1=cast(rwkv_a1), rwkv_a2=cast(rwkv_a2),
             rwkv_g1=cast(rwkv_g1), rwkv_g2=cast(rwkv_g2), rwkv_k_k=rwkv_k_k, rwkv_k_a=rwkv_k_a,
             rwkv_r_k=rwkv_r_k, rwkv_ln_g=rwkv_ln_g, rwkv_ln_b=rwkv_ln_b, rwkv_w_out=cast(rwkv_w_out))
    probs = jax.nn.softmax(hgrn_lb_logits.astype(F32), axis=0)
    lower = jnp.cumsum(probs, axis=0) - probs[0]

    b = x_prompt.shape[0]
    d = x_prompt.shape[2]
    n_a, n_c = state_hgrn.shape[0], state_wkv.shape[0]
    hgrn0 = jnp.zeros((n_a, b) + state_hgrn.shape[2:], F32)
    wkv0 = jnp.zeros((n_c, b) + state_wkv.shape[2:], F32)
    shift0 = jnp.zeros((n_c, b, 1, d), F32)
    y_p, hgrn_p, k_p, v_p, wkv_p, shift_p = _trunk(x_prompt, hgrn0, None, None, wkv0, shift0, w, lower)
    y_s, hgrn_s, k_s, v_s, wkv_s, shift_s = _trunk(x_sample, state_hgrn, cache_k_band, cache_v_band,
                                                   state_wkv, state_shift, w, lower)
    return (y_p, y_s, hgrn_p, hgrn_s, k_p, v_p, k_s, v_s, wkv_p, wkv_s, shift_p, shift_s)
```

```python
import functools
import math

import numpy as np
import jax
import jax.numpy as jnp
from jax import lax
from jax.experimental import pallas as pl
from jax.experimental.pallas import tpu as pltpu

F32 = jnp.float32
MXU_DTYPE = jnp.bfloat16

LANES = 128
CHUNK = 64
PREV_CHUNKS = 8
BAND = (PREV_CHUNKS + 1) * CHUNK
REL_CLIP = 128
NORM_EPS = 1e-6
RWKV_GN_EPS = 64e-5
HGRN_HEAD = 128
HGRN_BLOCKS_IN_FLIGHT = 4
HGRN_FINE_LEVELS = (8, 4)
ATTN_HEAD_DIM = 64
ATTN_CHUNKS_IN_FLIGHT = 4
RWKV_HEAD = 64
RWKV_CHUNKS_IN_FLIGHT = 4
VMEM_LIMIT =56 * 1024 * 1024


def _params(*sem):
    return pltpu.CompilerParams(dimension_semantics=sem, vmem_limit_bytes=VMEM_LIMIT)


def _tile(n, pref):
    if n <= pref:
        return n
    t = pref
    while n % t:
        t //= 2
    return t


def _mm(a, b):
    return jnp.dot(a.astype(MXU_DTYPE), b.astype(MXU_DTYPE), preferred_element_type=F32)


def _mm_nt(a, b):
    return lax.dot_general(a.astype(MXU_DTYPE), b.astype(MXU_DTYPE),
                           (((1,), (1,)), ((), ())), preferred_element_type=F32)


def _mm_tn(a, b):
    return lax.dot_general(a.astype(MXU_DTYPE), b.astype(MXU_DTYPE),
                           (((0,), (0,)), ((), ())), preferred_element_type=F32)


def _split3(x):
    hi = x.astype(jnp.bfloat16)
    r1 = x - hi.astype(F32)
    mid = r1.astype(jnp.bfloat16)
    lo = (r1 - mid.astype(F32)).astype(jnp.bfloat16)
    return hi, mid, lo


def _sel_mm(sel, x):
    sel = sel.astype(jnp.bfloat16)
    hi, mid, lo = _split3(x)
    d = lambda p: jnp.dot(sel, p, preferred_element_type=F32)
    return d(hi) + (d(mid) + d(lo))


def _lockstep(gens):
    out = [None] * len(gens)
    live = list(range(len(gens)))
    while live:
        for n in list(live):
            try:
                next(gens[n])
            except StopIteration as stop:
                out[n] = stop.value
                live.remove(n)
    return out


def _rms(x):
    return x * lax.rsqrt(jnp.mean(x * x, axis=-1, keepdims=True) + NORM_EPS)


def _sigmoid(x):
    return 1.0 / (1.0 + jnp.exp(-x))


def _silu(x):
    return x * _sigmoid(x)


def _ffn_up_kernel(x_ref, g_ref, wg_ref, wu_ref, h_ref, xn_ref):
    @pl.when(pl.program_id(1) == 0)
    def _():
        xn_ref[...] = (_rms(x_ref[...]) * g_ref[...]).astype(xn_ref.dtype)

    xn = xn_ref[...]
    gate = jnp.dot(xn, wg_ref[...], preferred_element_type=F32)
    up = jnp.dot(xn, wu_ref[...], preferred_element_type=F32)
    h_ref[...] = (_silu(gate) * up).astype(h_ref.dtype)


def _ffn(x, g, w_gate, w_up, w_d):
    t, d = x.shape
    f = w_d.shape[0]
    tf = min(f, 512)
    tm = _tile(t, 1024)
    h = pl.pallas_call(
        _ffn_up_kernel,
        grid=(t // tm, pl.cdiv(f, tf)),
        in_specs=[
            pl.BlockSpec((tm, d), lambda i, j: (i, 0)),
            pl.BlockSpec((1, d), lambda i, j: (0, 0)),
            pl.BlockSpec((d, tf), lambda i, j: (0, j)),
            pl.BlockSpec((d, tf), lambda i, j: (0, j)),
        ],
        out_specs=pl.BlockSpec((tm, tf), lambda i, j: (i, j)),
        out_shape=jax.ShapeDtypeStruct((t, f), MXU_DTYPE),
        scratch_shapes=[pltpu.VMEM((tm, d), MXU_DTYPE)],
        compiler_params=_params("parallel", "arbitrary"),
        name="ffn_up",
    )(x, g.reshape(1, d), w_gate, w_up)
    return _proj_res(x, h, w_d, scale=0.5)


def _prep_ffn_weights(w_gate_up, w_down):
    f = w_down.shape[0]
    return (w_gate_up[:, :f].astype(MXU_DTYPE), w_gate_up[:, f:].astype(MXU_DTYPE),
            w_down.astype(MXU_DTYPE))


def _norm_mm_kernel(x_ref, g_ref, w_ref, o_ref, xn_ref):
    @pl.when(pl.program_id(1) == 0)
    def _():
        xn_ref[...] = (_rms(x_ref[...]) * g_ref[...]).astype(xn_ref.dtype)

    o_ref[...] = jnp.dot(xn_ref[...], w_ref[...], preferred_element_type=F32)


def _norm_mm(x, g, w):
    t, d = x.shape
    n = w.shape[1]
    tm, tn = _tile(t, 1024), _tile(n, 1024)
    return pl.pallas_call(
        _norm_mm_kernel,
        grid=(t // tm, n // tn),
        in_specs=[
            pl.BlockSpec((tm, d), lambda i, j: (i, 0)),
            pl.BlockSpec((1, d), lambda i, j: (0, 0)),
            pl.BlockSpec((d, tn), lambda i, j: (0, j)),
        ],
        out_specs=pl.BlockSpec((tm, tn), lambda i, j: (i, j)),
        out_shape=jax.ShapeDtypeStruct((t, n), F32),
        scratch_shapes=[pltpu.VMEM((tm, d), MXU_DTYPE)],
        compiler_params=_params("parallel", "arbitrary"),
        name="norm_mm",
    )(x, g.reshape(1, d), w)


def _norm_kernel(x_ref, g_ref, o_ref):
    o_ref[...] = _rms(x_ref[...]) * g_ref[...]


def _norm(x, g):
    t, d = x.shape
    tm = _tile(t, 512)
    return pl.pallas_call(
        _norm_kernel,
        grid=(t // tm,),
        in_specs=[pl.BlockSpec((tm, d), lambda i: (i, 0)), pl.BlockSpec((1, d), lambda i: (0, 0))],
        out_specs=pl.BlockSpec((tm, d), lambda i: (i, 0)),
        out_shape=jax.ShapeDtypeStruct((t, d), F32),
        compiler_params=_params("parallel"),
        name="norm",
    )(x, g.reshape(1, d))


def _proj_res_kernel(x_ref, a_ref, w_ref, o_ref, *, scale):
    y = _mm(a_ref[...], w_ref[...])
    o_ref[...] = x_ref[...] + (y if scale == 1.0 else scale * y)


def _gated_proj_res_kernel(x_ref, a_ref, z_ref, g_ref, w_ref, o_ref, *, sub):
    for r in range(0, x_ref.shape[0], sub):
        rows = slice(r, r + sub)
        an = _rms(a_ref[rows, :]) * g_ref[...] * _silu(z_ref[rows, :])
        o_ref[rows, :] = x_ref[rows, :] + _mm(an, w_ref[...])


def _proj_res(x, a, w, scale=1.0):
    t, d = x.shape
    k = a.shape[1]
    tm, tn = _tile(t, 512), _tile(d, 1024)
    return pl.pallas_call(
        functools.partial(_proj_res_kernel, scale=scale),
        grid=(t // tm, d // tn),
        in_specs=[
            pl.BlockSpec((tm, tn), lambda i, j: (i, j)),
            pl.BlockSpec((tm, k), lambda i, j: (i, 0)),
            pl.BlockSpec((k, tn), lambda i, j: (0, j)),
        ],
        out_specs=pl.BlockSpec((tm, tn), lambda i, j: (i, j)),
        out_shape=jax.ShapeDtypeStruct((t, d), F32),
        compiler_params=_params("parallel", "arbitrary"),
        name="proj_res",
    )(x, a, w)


def _gated_proj_res(x, a, z_src, z_col, g, w):
    t, d = x.shape
    tm = _tile(t, 512)
    row = lambda col: pl.BlockSpec((tm, d), lambda i: (i, col))
    return pl.pallas_call(
        functools.partial(_gated_proj_res_kernel, sub=_tile(tm, 128)),
        grid=(t // tm,),
        in_specs=[row(0), row(0), row(z_col),
                  pl.BlockSpec((1, d), lambda i: (0, 0)),
                  pl.BlockSpec((d, d), lambda i: (0, 0))],
        out_specs=row(0),
        out_shape=jax.ShapeDtypeStruct((t, d), F32),
        compiler_params=_params("parallel"),
        name="gated_proj_res",
    )(x, a, z_src, g.reshape(1, d), w)


def _hgrn_levels(rows):
    t = np.arange(rows)[:, None]
    r = np.arange(rows)[None, :]
    mats = [r <= t]
    for b in HGRN_FINE_LEVELS:
        mid = (t // b) * b + b // 2 - 1
        upper = (t % b) >= b // 2
        mats.append(np.where(upper, (r > mid) & (r <= t), (r > t) & (r <= mid)))
    return np.concatenate(mats, axis=0).astype(np.float32)


def _hgrn_kernel(q_ref, f_ref, v_ref, lb_ref, lvl_ref, s0_ref, o_ref, sout_ref, st_ref, *, rows, nblk):
    i = pl.program_id(2)

    @pl.when(i == 0)
    def _():
        st_ref[...] = s0_ref[0, 0]

    lb = lb_ref[...]
    lvl = lvl_ref[...]
    t_id = lax.broadcasted_iota(jnp.int32, (rows, HGRN_HEAD), 0)
    ti = lax.broadcasted_iota(jnp.int32, (rows, rows), 0)
    si = lax.broadcasted_iota(jnp.int32, (rows, rows), 1)
    scale = HGRN_HEAD ** -0.5

    def level(scores, b, qe, ke):
        upper = (t_id & (b - 1)) >= b // 2
        s_l = _mm_nt(jnp.where(upper, qe, 0.0), jnp.where(upper, 0.0, ke))
        if b < rows:
            sh = int(math.log2(b))
            s_l = jnp.where((ti >> sh) == (si >> sh), s_l, 0.0)
        return scores + s_l

    def block(sl):
        q = _silu(q_ref[sl, :]) * scale
        fz = f_ref[sl, :]
        v = v_ref[sl, :]
        f = lb + (1.0 - lb) * _sigmoid(fz)
        k = (1.0 - lb) * _sigmoid(-fz)
        sums = _sel_mm(lvl, jnp.log(f))
        yield
        g = sums[:rows]
        scores = jnp.where(ti == si, _mm_nt(q, k), 0.0)
        yield
        scores = level(scores, 2, q * f, k)
        yield
        for n, b in enumerate(HGRN_FINE_LEVELS):
            if b <= rows:
                e = jnp.exp(sums[(n + 1) * rows:(n + 2) * rows])
                scores = level(scores, b, q * e, k * e)
                yield
        b = 2 * HGRN_FINE_LEVELS[0]
        while b <= rows:
            g_mid = jnp.concatenate(
                [jnp.broadcast_to(g[m:m + 1, :], (b, HGRN_HEAD)) for m in range(b // 2 - 1, rows, b)], axis=0)
            upper = (t_id & (b - 1)) >= b // 2
            e = jnp.exp(jnp.where(upper, g - g_mid, g_mid - g))
            scores = level(scores, b, q * e, k * e)
            yield
            b *= 2
        g_last = g[rows - 1:rows, :]
        o_intra = _mm(scores, v)
        yield
        kv = _mm_tn(v, k * jnp.exp(g_last - g))
        return o_intra, q * jnp.exp(g), jnp.exp(g_last), kv

    st = st_ref[...]
    for u0 in range(0, nblk, HGRN_BLOCKS_IN_FLIGHT):
        slices = [slice(u * rows, (u + 1) * rows) for u in range(u0, min(u0 + HGRN_BLOCKS_IN_FLIGHT, nblk))]
        blocks = _lockstep([block(sl) for sl in slices])
        for sl, (o_intra, qg, dec, kv) in zip(slices, blocks):
            o_ref[sl, :] = o_intra + _mm_nt(qg, st)
            st = st * dec + kv
    st_ref[...] = st

    @pl.when(i == pl.num_programs(2) - 1)
    def _():
        sout_ref[0, 0] = st


def _hgrn_recurrence(proj, lb, s0_t, b, t):
    d = proj.shape[1] // 4
    h = d // HGRN_HEAD
    tb = _tile(t, 1024)
    nb = t // tb
    rows = _tile(tb, 128)
    lvl = jnp.asarray(_hgrn_levels(rows), dtype=jnp.bfloat16)
    kern = functools.partial(_hgrn_kernel, rows=rows, nblk=tb // rows)
    row = lambda bi, hi, i: bi * nb + i
    return pl.pallas_call(
        kern,
        grid=(b, h, nb),
        in_specs=[
            pl.BlockSpec((tb, HGRN_HEAD), lambda bi, hi, i: (row(bi, hi, i), hi)),
            pl.BlockSpec((tb, HGRN_HEAD), lambda bi, hi, i: (row(bi, hi, i), h + hi)),
            pl.BlockSpec((tb, HGRN_HEAD), lambda bi, hi, i: (row(bi, hi, i), 2 * h + hi)),
            pl.BlockSpec((1, HGRN_HEAD), lambda bi, hi, i: (0, hi)),
            pl.BlockSpec(lvl.shape, lambda bi, hi, i: (0, 0)),
            pl.BlockSpec((1, 1, HGRN_HEAD, HGRN_HEAD), lambda bi, hi, i: (bi, hi, 0, 0)),
        ],
        out_specs=[
            pl.BlockSpec((tb, HGRN_HEAD), lambda bi, hi, i: (row(bi, hi, i), hi)),
            pl.BlockSpec((1, 1, HGRN_HEAD, HGRN_HEAD), lambda bi, hi, i: (bi, hi, 0, 0)),
        ],
        out_shape=[jax.ShapeDtypeStruct((b * t, d), F32),
                   jax.ShapeDtypeStruct((b, h, HGRN_HEAD, HGRN_HEAD), F32)],
        scratch_shapes=[pltpu.VMEM((HGRN_HEAD, HGRN_HEAD), F32)],
        compiler_params=_params("parallel", "parallel", "arbitrary"),
        name="hgrn_recurrence",
    )(proj, proj, proj, lb.reshape(1, d), lvl, s0_t)


def _softmax_pv(parts):
    m = functools.reduce(jnp.maximum, [jnp.max(s, axis=-1, keepdims=True) for s, _ in parts])
    es = [jnp.exp(s - m) for s, _ in parts]
    den = functools.reduce(jnp.add, [jnp.sum(e, axis=-1, keepdims=True) for e in es])
    yield
    pv = functools.reduce(jnp.add, [_mm(e, v) for e, (_, v) in zip(es, parts)])
    yield
    return pv / den


def _attn_prompt_kernel(q_ref, kp_ref, kc_ref, vp_ref, vc_ref, bias_ref, o_ref, k_scr, v_scr, *, tq):
    i = pl.program_id(2)
    k_scr[0:tq, :] = kp_ref[...].astype(k_scr.dtype)
    k_scr[tq:2 * tq, :] = kc_ref[...].astype(k_scr.dtype)
    v_scr[0:tq, :] = vp_ref[...].astype(v_scr.dtype)
    v_scr[tq:2 * tq, :] = vc_ref[...].astype(v_scr.dtype)
    lane = lax.broadcasted_iota(jnp.int32, (CHUNK, LANES), 1)
    head_a = lane < ATTN_HEAD_DIM
    col = lax.broadcasted_iota(jnp.int32, (CHUNK, BAND), 1)
    scale = ATTN_HEAD_DIM ** -0.5
    past = PREV_CHUNKS * CHUNK

    def unit(j, hd, masked):
        q = q_ref[j * CHUNK:(j + 1) * CHUNK, :] * scale
        w0 = tq - past + j * CHUNK
        sel = head_a if hd == 0 else jnp.logical_not(head_a)
        s = _mm_nt(jnp.where(sel, q, 0.0), k_scr[w0:w0 + BAND, :]) + bias_ref[hd]
        yield
        if masked:
            s = jnp.where(col + ((i - 1) * tq + w0) >= 0, s, -jnp.inf)
        return (yield from _softmax_pv([(s, v_scr[w0:w0 + BAND, :])]))

    def run(masked):
        for j0 in range(0, tq // CHUNK, ATTN_CHUNKS_IN_FLIGHT):
            js = range(j0, min(j0 + ATTN_CHUNKS_IN_FLIGHT, tq // CHUNK))
            outs = _lockstep([unit(j, hd, masked) for j in js for hd in range(2)])
            for n, j in enumerate(js):
                o_ref[j * CHUNK:(j + 1) * CHUNK, :] = jnp.where(head_a, outs[2 * n], outs[2 * n + 1])

    @pl.when(i == 0)
    def _():
        run(True)

    @pl.when(i > 0)
    def _():
        run(False)


def _attn_prompt(qkv, bias, b, t):
    d = qkv.shape[1] // 3
    npair = d // LANES
    tq = _tile(t, 512)
    nq = t // tq
    assert tq >= PREV_CHUNKS * CHUNK and tq % CHUNK == 0
    cur = lambda bi, i: bi * nq + i
    prev = lambda bi, i: bi * nq + jnp.maximum(i - 1, 0)
    return pl.pallas_call(
        functools.partial(_attn_prompt_kernel, tq=tq),
        grid=(b, npair, nq),
        in_specs=[
            pl.BlockSpec((tq, LANES), lambda bi, p, i: (cur(bi, i), p)),
            pl.BlockSpec((tq, LANES), lambda bi, p, i: (prev(bi, i), npair + p)),
            pl.BlockSpec((tq, LANES), lambda bi, p, i: (cur(bi, i), npair + p)),
            pl.BlockSpec((tq, LANES), lambda bi, p, i: (prev(bi, i), 2 * npair + p)),
            pl.BlockSpec((tq, LANES), lambda bi, p, i: (cur(bi, i), 2 * npair + p)),
            pl.BlockSpec((2, CHUNK, BAND), lambda bi, p, i: (p, 0, 0)),
        ],
        out_specs=pl.BlockSpec((tq, LANES), lambda bi, p, i: (cur(bi, i), p)),
        out_shape=jax.ShapeDtypeStruct((b * t, d), F32),
        scratch_shapes=[pltpu.VMEM((2 * tq, LANES), MXU_DTYPE), pltpu.VMEM((2 * tq, LANES), MXU_DTYPE)],
        compiler_params=_params("parallel", "parallel", "arbitrary"),
        name="attn_prompt",
    )(qkv, qkv, qkv, qkv, qkv, bias)


def _attn_cached_kernel(q_ref, kn_ref, vn_ref, kc_ref, vc_ref, bc_ref, bn_ref, o_ref):
    q = q_ref[...]
    lane = lax.broadcasted_iota(jnp.int32, q.shape, 1)
    head_a = lane < ATTN_HEAD_DIM
    scale = ATTN_HEAD_DIM ** -0.5
    kc, vc, kn, vn = kc_ref[0], vc_ref[0], kn_ref[...], vn_ref[...]

    def unit(hd, sel):
        qm = jnp.where(sel, q * scale, 0.0)
        s_c = _mm_nt(qm, kc) + bc_ref[hd]
        s_n = _mm_nt(qm, kn) + bn_ref[hd]
        return (yield from _softmax_pv([(s_c, vc), (s_n, vn)]))

    outs = _lockstep([unit(0, head_a), unit(1, jnp.logical_not(head_a))])
    o_ref[...] = jnp.where(head_a, outs[0], outs[1])


def _attn_cached(qkv, k_cache, v_cache, bias_c, bias_n, b, t):
    d = qkv.shape[1] // 3
    npair = d // LANES
    rows = k_cache.shape[1]
    return pl.pallas_call(
        _attn_cached_kernel,
        grid=(b, npair),
        in_specs=[
            pl.BlockSpec((t, LANES), lambda bi, p: (bi, p)),
            pl.BlockSpec((t, LANES), lambda bi, p: (bi, npair + p)),
            pl.BlockSpec((t, LANES), lambda bi, p: (bi, 2 * npair + p)),
            pl.BlockSpec((1, rows, LANES), lambda bi, p: (bi, 0, p)),
            pl.BlockSpec((1, rows, LANES), lambda bi, p: (bi, 0, p)),
            pl.BlockSpec((2, t, rows), lambda bi, p: (p, 0, 0)),
            pl.BlockSpec((2, t, t), lambda bi, p: (p, 0, 0)),
        ],
        out_specs=pl.BlockSpec((t, LANES), lambda bi, p: (bi, p)),
        out_shape=jax.ShapeDtypeStruct((b * t, d), F32),
        compiler_params=_params("parallel", "parallel"),
        name="attn_cached",
    )(qkv, qkv, qkv, k_cache, v_cache, bias_c, bias_n)


def _rel_bias(rel_bias, nq, nk, offset):
    rel = jnp.arange(nq + nk - 1) + (offset - nk + 1)
    diag = rel_bias[:, jnp.clip(rel, -REL_CLIP, REL_CLIP) + REL_CLIP].astype(F32)
    rev = diag[:, ::-1]
    return jnp.stack([rev[:, nq - 1 - q:nq - 1 - q + nk] for q in range(nq)], axis=1)


def _rwkv_rkv_kernel(h_ref, p_ref, mu_ref, w_ref, o_ref, l_ref):
    @pl.when(pl.program_id(2) == 0)
    def _():
        h = h_ref[...]
        l_ref[...] = (h + (p_ref[...] - h) * mu_ref[0]).astype(l_ref.dtype)

    o_ref[0] = jnp.dot(l_ref[...], w_ref[0], preferred_element_type=F32)


def _rwkv_rkv(h, h_prev, mu, w_rkv):
    t, d = h.shape
    tm, tn = _tile(t, 512), _tile(d, 2048)
    return pl.pallas_call(
        _rwkv_rkv_kernel,
        grid=(3, t // tm, d // tn),
        in_specs=[
            pl.BlockSpec((tm, d), lambda c, i, j: (i, 0)),
            pl.BlockSpec((tm, d), lambda c, i, j: (i, 0)),
            pl.BlockSpec((1, 1, d), lambda c, i, j: (c, 0, 0)),
            pl.BlockSpec((1, d, tn), lambda c, i, j: (c, 0, j)),
        ],
        out_specs=pl.BlockSpec((1, tm, tn), lambda c, i, j: (c, i, j)),
        out_shape=jax.ShapeDtypeStruct((3, t, d), F32),
        scratch_shapes=[pltpu.VMEM((tm, d), MXU_DTYPE)],
        compiler_params=_params("parallel", "parallel", "arbitrary"),
        name="rwkv_rkv",
    )(h, h_prev, mu.reshape(-1, 1, d), w_rkv)


def _softplus(y):
    return jnp.maximum(y, 0.0) + jnp.log1p(jnp.exp(-jnp.abs(y)))


def _rwkv_lora_kernel(h_ref, p_ref, mu_ref, w0_ref, w1_ref, w2_ref, a0_ref, a1_ref, a2_ref,
                      g1_ref, g2_ref, lw_ref, a_ref, gate_ref):
    h = h_ref[...]
    xx = p_ref[...] - h
    lerp = lambda c: h + xx * mu_ref[c]
    z = w0_ref[...] + _mm(jnp.tanh(_mm(lerp(0), w1_ref[...])), w2_ref[...])
    w_log = -_softplus(-z) - 0.5
    lw_ref[...] = -jnp.exp(w_log)
    a_ref[...] = _sigmoid(a0_ref[...] + _mm(_mm(lerp(1), a1_ref[...]), a2_ref[...]))
    gate_ref[...] = _mm(_sigmoid(_mm(lerp(2), g1_ref[...])), g2_ref[...])


def _rwkv_lora(h, h_prev, mu_wag, w0, w1, w2, a0, a1, a2, g1, g2):
    t, d = h.shape
    tm = _tile(t, 256)
    row = pl.BlockSpec((tm, d), lambda i: (i, 0))
    full = lambda a: pl.BlockSpec(a.shape, lambda i: (0,) * a.ndim)
    consts = [mu_wag.reshape(3, 1, d), w0.reshape(1, d), w1, w2, a0.reshape(1, d), a1, a2, g1, g2]
    return pl.pallas_call(
        _rwkv_lora_kernel,
        grid=(t // tm,),
        in_specs=[row, row] + [full(c) for c in consts],
        out_specs=[row, row, row],
        out_shape=[jax.ShapeDtypeStruct((t, d), F32)] * 3,
        compiler_params=_params("parallel"),
        name="rwkv_lora",
    )(h, h_prev, *consts)


def _rwkv_kernel(r_ref, k_ref, v_ref, lw_ref, a_ref, gate_ref, kkw_ref, kaw_ref, rkw_ref,
                 lng_ref, lnb_ref, s0_ref, o_ref, sout_ref, s_ref, *, c_rows, nchunk, npb, ngrp):
    i = pl.program_id(2)
    c2 = 2 * c_rows

    @pl.when(i == 0)
    def _():
        s_ref[...] = s0_ref[...]

    lane = lax.broadcasted_iota(jnp.int32, (c_rows, LANES), 1)
    head_a = lane < RWKV_HEAD
    ri = lax.broadcasted_iota(jnp.int32, (c_rows, c_rows), 0)
    rj = lax.broadcasted_iota(jnp.int32, (c_rows, c_rows), 1)
    tri = (rj <= ri).astype(F32)
    si = lax.broadcasted_iota(jnp.int32, (c2, c2), 0)
    sj = lax.broadcasted_iota(jnp.int32, (c2, c2), 1)
    same_blk = (si // c_rows) == (sj // c_rows)
    strict = jnp.logical_and(same_blk, (sj % c_rows) < (si % c_rows))
    incl = jnp.logical_and(same_blk, (sj % c_rows) <= (si % c_rows))
    inv_n = 1.0 / RWKV_HEAD

    def stack(x):
        return jnp.concatenate([jnp.where(head_a, x, 0.0), jnp.where(head_a, 0.0, x)], axis=0)

    def head_sum(x):
        sa = jnp.sum(jnp.where(head_a, x, 0.0), axis=-1, keepdims=True)
        sb = jnp.sum(jnp.where(head_a, 0.0, x), axis=-1, keepdims=True)
        return jnp.where(head_a, sa, sb)

    def prepare(rows, lanes):
        r, kr, v = r_ref[rows, lanes], k_ref[rows, lanes], v_ref[rows, lanes]
        lw, a = lw_ref[rows, lanes], a_ref[rows, lanes]
        kkw, kaw = kkw_ref[:, lanes], kaw_ref[:, lanes]
        kk = kr * kkw
        ss = head_sum(kk * kk)
        cl = _sel_mm(tri, lw)
        yield
        kk = kk * lax.rsqrt(jnp.maximum(ss, 1e-24))
        k = kr * (1.0 + (a - 1.0) * kaw)
        e_neg = jnp.exp(-cl)
        al = stack(-kk * jnp.exp(cl - lw))
        rt = stack(r * jnp.exp(cl))
        bg = stack(a * kk * e_neg)
        rhs = jnp.concatenate([bg, stack(k * e_neg)], axis=0)
        aa = _mm_nt(jnp.concatenate([al, rt], axis=0), rhs)
        yield
        n_ab = jnp.where(strict, aa[:c2, :c2], 0.0)
        a_ak = jnp.where(strict, aa[:c2, c2:], 0.0)
        a_rb = jnp.where(incl, aa[c2:, :c2], 0.0)
        a_rk = jnp.where(incl, aa[c2:, c2:], 0.0)
        vs = stack(v)
        x = jnp.concatenate([al, _mm(a_ak, vs)], axis=1)
        yield
        pw = n_ab
        x = x + _mm(pw, x)
        yield
        for _ in range(int(math.log2(c_rows)) - 1):
            pw = _mm(pw, pw)
            yield
            x = x + _mm(pw, x)
            yield
        w, u0 = x[:, :LANES], x[:, LANES:]
        y = _mm(a_rb, x)
        yield
        ro = rt + y[:, :LANES]
        o0 = y[:, LANES:] + _mm(a_rk, vs)
        yield
        p = _mm_tn(w, bg)
        yield
        q = _mm_tn(jnp.concatenate([u0, vs], axis=0), rhs)
        yield
        gam = jnp.exp(cl[c_rows - 1:c_rows, :])
        bonus = head_sum(r * k * rkw_ref[:, lanes]) * v
        return ro, o0, p, q, gam, bonus

    def finish(rows, lanes, o2, bonus):
        o = o2[:c_rows] + o2[c_rows:]
        mean = head_sum(o) * inv_n
        dlt = o - mean
        var = head_sum(dlt * dlt) * inv_n
        on =dlt * lax.rsqrt(var + RWKV_GN_EPS) * lng_ref[:, lanes] + lnb_ref[:, lanes]
        o_ref[rows, lanes] = (on + bonus) * gate_ref[rows, lanes]

    def group(g, carry):
        r0 = pl.multiple_of(g * (ngrp * c_rows), ngrp * c_rows)
        lanes = [slice(pi * LANES, (pi + 1) * LANES) for pi in range(npb)]
        rows = [pl.ds(r0 + c * c_rows, c_rows) for c in range(ngrp)]
        units = [(c, pi) for c in range(ngrp) for pi in range(npb)]
        prep = dict(zip(units, _lockstep([prepare(rows[c], lanes[pi]) for c, pi in units])))
        s = [s_ref[pi] for pi in range(npb)]
        for c, pi in units:
            ro, o0, p, q, gam, bonus = prep[c, pi]
            finish(rows[c], lanes[pi], _mm_nt(ro, s[pi]) + o0, bonus)
            s[pi] = (s[pi] + _mm(s[pi], p) + q) * gam
        for pi in range(npb):
            s_ref[pi] = s[pi]
        return carry

    lax.fori_loop(0, nchunk // ngrp, group, 0)

    @pl.when(i == pl.num_programs(2) - 1)
    def _():
        sout_ref[...] = s_ref[...]


def _rwkv_recurrence(rkv, lw, a, gate, kkw, kaw, rkw, lng, lnb, s0_blk, b, t):
    d = lw.shape[1]
    npair = d // LANES
    c_rows = min(CHUNK, t)
    tb = _tile(t, 512)
    nb = t // tb
    nchunk = tb // c_rows
    npb = 2 if npair % 2 == 0 else 1
    ngrp = _tile(nchunk, RWKV_CHUNKS_IN_FLIGHT)
    bw = npb * LANES
    kern = functools.partial(_rwkv_kernel, c_rows=c_rows, nchunk=nchunk, npb=npb, ngrp=ngrp)
    rowblk = pl.BlockSpec((tb, bw), lambda bi, p, i: (bi * nb + i, p))
    rkvblk = lambda c: pl.BlockSpec((None, tb, bw), lambda bi, p, i: (c, bi * nb + i, p))
    vec = pl.BlockSpec((1, bw), lambda bi, p, i: (0, p))
    st = pl.BlockSpec((None, npb, LANES, LANES), lambda bi, p, i: (bi, p, 0, 0))
    return pl.pallas_call(
        kern,
        grid=(b, npair // npb, nb),
        in_specs=[rkvblk(0), rkvblk(1), rkvblk(2), rowblk, rowblk, rowblk, vec, vec, vec, vec, vec, st],
        out_specs=[rowblk, st],
        out_shape=[jax.ShapeDtypeStruct((b * t, d), F32),
                   jax.ShapeDtypeStruct((b, npair, LANES, LANES), F32)],
        scratch_shapes=[pltpu.VMEM((npb, LANES, LANES), F32)],
        compiler_params=_params("parallel", "parallel", "arbitrary"),
        name="rwkv_recurrence",
    )(rkv, rkv, rkv, lw, a, gate, kkw.reshape(1, d), kaw.reshape(1, d), rkw.reshape(1, d),
      lng.reshape(1, d), lnb.reshape(1, d), s0_blk)


def _to_blockdiag(s):
    b, h, n, _ = s.shape
    s = s.reshape(b, h // 2, 2, n, n)
    z = jnp.zeros_like(s[:, :, 0])
    top = jnp.concatenate([s[:, :, 0], z], axis=-1)
    bot = jnp.concatenate([z, s[:, :, 1]], axis=-1)
    return jnp.concatenate([top, bot], axis=-2)


def _from_blockdiag(sb):
    b, p, n2, _ = sb.shape
    n = n2 // 2
    return jnp.stack([sb[:, :, :n, :n], sb[:, :, n:, n:]], axis=2).reshape(b, 2 * p, n, n)


def _trunk(x3, hgrn_s0, k_cache, v_cache, wkv_s0, shift0, w, lower):
    b, t, d = x3.shape
    x = x3.reshape(b * t, d)
    depth = w['norm_g'].shape[0]
    hgrn_out, k_out, v_out, wkv_out, shift_out = [], [], [], [], []
    for layer in range(depth):
        kind, j = layer % 3, layer // 3
        g = w['norm_g'][layer]
        x = _ffn(x, g[0], *w['ffn'][layer][0])
        if kind == 0:
            proj = _norm_mm(x, g[1], w['hgrn_w_in'][j])
            s0_t = jnp.swapaxes(hgrn_s0[j], -1, -2)
            o, s_t = _hgrn_recurrence(proj, lower[layer], s0_t, b, t)
            hgrn_out.append(jnp.swapaxes(s_t, -1, -2))
            x = _gated_proj_res(x, o, proj, 3, w['hgrn_norm_g'][j], w['hgrn_w_out'][j])
        elif kind == 1:
            qkv = _norm_mm(x, g[1], w['attn_w_qkv'][j])
            rel_bias = w['attn_rel_bias'][j]
            nh = rel_bias.shape[0]
            if k_cache is None:
                past = PREV_CHUNKS * CHUNK
                o = _attn_prompt(qkv, _rel_bias(rel_bias, CHUNK, BAND, past), b, t)
                kept = min(past, t)
            else:
                rows = k_cache.shape[2]
                bias = _rel_bias(rel_bias, t, rows + t, rows)
                o = _attn_cached(qkv, k_cache[j].reshape(b, rows, d), v_cache[j].reshape(b, rows, d),
                                 bias[:, :, :rows], bias[:, :, rows:], b, t)
                kept = t
            qkv3 = qkv.reshape(b, t, 3 * d)
            keep = lambda c: qkv3[:, t - kept:, c * d:(c + 1) * d].reshape(b, kept, nh, ATTN_HEAD_DIM)
            k_out.append(keep(1))
            v_out.append(keep(2))
            x = _proj_res(x, o, w['attn_w_out'][j])
        else:
            h = _norm(x, g[1])
            h3 = h.reshape(b, t, d)
            h_prev = jnp.concatenate([shift0[j].astype(F32), h3[:, :-1]], axis=1).reshape(b * t, d)
            mu = w['rwkv_mu'][j]
            rkv = _rwkv_rkv(h, h_prev, mu[:3], w['rwkv_w_rkv'][j])
            lw, a, gate = _rwkv_lora(h, h_prev, mu[3:], w['rwkv_w0'][j], w['rwkv_w1'][j], w['rwkv_w2'][j],
                                     w['rwkv_a0'][j], w['rwkv_a1'][j], w['rwkv_a2'][j],
                                     w['rwkv_g1'][j], w['rwkv_g2'][j])
            o, s_blk = _rwkv_recurrence(rkv, lw, a, gate, w['rwkv_k_k'][j], w['rwkv_k_a'][j],
                                        w['rwkv_r_k'][j].reshape(-1), w['rwkv_ln_g'][j], w['rwkv_ln_b'][j],
                                        _to_blockdiag(wkv_s0[j]), b, t)
            shift_out.append(h3[:, -1:])
            wkv_out.append(_from_blockdiag(s_blk))
            x = _proj_res(x, o, w['rwkv_w_out'][j])
        x = _ffn(x, g[2], *w['ffn'][layer][1])
    y = _norm(x, w['final_norm_g']).reshape(b, t, d)
    return (y, jnp.stack(hgrn_out), jnp.stack(k_out), jnp.stack(v_out),
            jnp.stack(wkv_out), jnp.stack(shift_out))


def kernel(x_prompt, x_sample, state_hgrn, cache_k_band, cache_v_band, state_wkv, state_shift,
           norm_g, final_norm_g, ffn_w_gate_up, ffn_w_down,
           hgrn_w_in, hgrn_lb_logits, hgrn_norm_g, hgrn_w_out,
           attn_w_qkv, attn_rel_bias, attn_w_out,
           rwkv_mu, rwkv_w_rkv, rwkv_w0, rwkv_w1, rwkv_w2, rwkv_a0, rwkv_a1, rwkv_a2,
           rwkv_g1, rwkv_g2, rwkv_k_k, rwkv_k_a, rwkv_r_k, rwkv_ln_g, rwkv_ln_b, rwkv_w_out):
    depth = norm_g.shape[0]
    cast = lambda a: a.astype(MXU_DTYPE)
    ffn = [[_prep_ffn_weights(ffn_w_gate_up[l, i], ffn_w_down[l, i]) for i in range(2)] for l in range(depth)]
    w = dict(norm_g=norm_g, final_norm_g=final_norm_g, ffn=ffn,
             hgrn_w_in=cast(hgrn_w_in), hgrn_norm_g=hgrn_norm_g, hgrn_w_out=cast(hgrn_w_out),
             attn_w_qkv=cast(attn_w_qkv), attn_rel_bias=attn_rel_bias, attn_w_out=cast(attn_w_out),
             rwkv_mu=rwkv_mu, rwkv_w_rkv=cast(rwkv_w_rkv), rwkv_w0=rwkv_w0, rwkv_w1=cast(rwkv_w1),
             rwkv_w2=cast(rwkv_w2), rwkv_a0=rwkv_a0, rwkv_a1=cast(rwkv_a1), rwkv_a2=cast(rwkv_a2),
             rwkv_g1=cast(rwkv_g1), rwkv_g2=cast(rwkv_g2), rwkv_k_k=rwkv_k_k, rwkv_k_a=rwkv_k_a,
             rwkv_r_k=rwkv_r_k, rwkv_ln_g=rwkv_ln_g, rwkv_ln_b=rwkv_ln_b, rwkv_w_out=cast(rwkv_w_out))
    probs = jax.nn.softmax(hgrn_lb_logits.astype(F32), axis=0)
    lower = jnp.cumsum(probs, axis=0) - probs[0]

    b = x_prompt.shape[0]
    d = x_prompt.shape[2]
    n_a, n_c = state_hgrn.shape[0], state_wkv.shape[0]
    hgrn0 = jnp.zeros((n_a, b) + state_hgrn.shape[2:], F32)
    wkv0 = jnp.zeros((n_c, b) + state_wkv.shape[2:], F32)
    shift0 = jnp.zeros((n_c, b, 1, d), F32)
    y_p, hgrn_p, k_p, v_p, wkv_p, shift_p = _trunk(x_prompt, hgrn0, None, None, wkv0, shift0, w, lower)
    y_s, hgrn_s, k_s, v_s, wkv_s, shift_s = _trunk(x_sample, state_hgrn, cache_k_band, cache_v_band,
                                                   state_wkv, state_shift, w, lower)
    return (y_p, y_s, hgrn_p, hgrn_s, k_p, v_p, k_s, v_s, wkv_p, wkv_s, shift_p, shift_s)
```

```python
import functools
import math

import numpy as np
import jax
import jax.numpy as jnp
from jax import lax
from jax.experimental import pallas as pl
from jax.experimental.pallas import tpu as pltpu

F32 = jnp.float32
MXU_DTYPE = jnp.bfloat16

LANES = 128
CHUNK = 64
PREV_CHUNKS = 8
BAND = (PREV_CHUNKS + 1) * CHUNK
REL_CLIP = 128
NORM_EPS = 1e-6
RWKV_GN_EPS = 64e-5
HGRN_HEAD = 128
HGRN_BLOCKS_IN_FLIGHT = 4
HGRN_FINE_LEVELS = (8, 4)
ATTN_HEAD_DIM = 64
ATTN_CHUNKS_IN_FLIGHT = 4
RWKV_HEAD = 64
RWKV_CHUNKS_IN_FLIGHT = 8
VMEM_LIMIT = 56 * 1024 * 1024


def _params(*sem):
    return pltpu.CompilerParams(dimension_semantics=sem, vmem_limit_bytes=VMEM_LIMIT)


def _tile(n, pref):
    if n <= pref:
        return n
    t = pref
    while n % t:
        t //= 2
    return t


def _mm(a, b):
    return jnp.dot(a.astype(MXU_DTYPE), b.astype(MXU_DTYPE), preferred_element_type=F32)


def _mm_nt(a, b):
    return lax.dot_general(a.astype(MXU_DTYPE), b.astype(MXU_DTYPE),
                           (((1,), (1,)), ((), ())), preferred_element_type=F32)


def _mm_tn(a, b):
    return lax.dot_general(a.astype(MXU_DTYPE), b.astype(MXU_DTYPE),
                           (((0,), (0,)), ((), ())), preferred_element_type=F32)


def _split3(x):
    hi = x.astype(jnp.bfloat16)
    r1 = x - hi.astype(F32)
    mid = r1.astype(jnp.bfloat16)
    lo = (r1 - mid.astype(F32)).astype(jnp.bfloat16)
    return hi, mid, lo


def _sel_mm(sel, x):
    sel = sel.astype(jnp.bfloat16)
    hi, mid, lo = _split3(x)
    d = lambda p: jnp.dot(sel, p, preferred_element_type=F32)
    return d(hi) + (d(mid) + d(lo))


def _cumsum_rows(x):
    row = lax.broadcasted_iota(jnp.int32, x.shape, 0)
    shift = 1
    while shift < x.shape[0]:
        x = x + jnp.where(row >= shift, pltpu.roll(x, shift, axis=0), 0.0)
        shift *= 2
    return x


def _lockstep(gens):
    out = [None] * len(gens)
    live = list(range(len(gens)))
    while live:
        for n in list(live):
            try:
                next(gens[n])
            except StopIteration as stop:
                out[n] = stop.value
                live.remove(n)
    return out


def _rms(x):
    return x * lax.rsqrt(jnp.mean(x * x, axis=-1, keepdims=True) + NORM_EPS)


def _sigmoid(x):
    return 1.0 / (1.0 + jnp.exp(-x))


def _silu(x):
    return x * _sigmoid(x)


def _ffn_up_kernel(x_ref, g_ref, wg_ref, wu_ref, h_ref, xn_ref, *, tail):
    j = pl.program_id(1)

    @pl.when(j == 0)
    def _():
        xn_ref[...] = (_rms(x_ref[...]) * g_ref[...]).astype(xn_ref.dtype)

    def columns(width):
        xn = xn_ref[...]
        gate = jnp.dot(xn, wg_ref[:, :width], preferred_element_type=F32)
        up = jnp.dot(xn, wu_ref[:, :width], preferred_element_type=F32)
        h_ref[:, :width] = (_silu(gate) * up).astype(h_ref.dtype)

    last = pl.num_programs(1) - 1
    pl.when(j < last)(lambda: columns(h_ref.shape[1]))
    pl.when(j == last)(lambda: columns(tail))


def _ffn(x, g, w_gate, w_up, w_d):
    t, d = x.shape
    f = w_d.shape[0]
    tf = min(f, 1024)
    nf = pl.cdiv(f, tf)
    tm = _tile(t, 1024)
    h = pl.pallas_call(
        functools.partial(_ffn_up_kernel, tail=f - (nf - 1) * tf),
        grid=(t // tm, nf),
        in_specs=[
            pl.BlockSpec((tm, d), lambda i, j: (i, 0)),
            pl.BlockSpec((1, d), lambda i, j: (0, 0)),
            pl.BlockSpec((d, tf), lambda i, j: (0, j)),
            pl.BlockSpec((d, tf), lambda i, j: (0, j)),
        ],
        out_specs=pl.BlockSpec((tm, tf), lambda i, j: (i, j)),
        out_shape=jax.ShapeDtypeStruct((t, f), MXU_DTYPE),
        scratch_shapes=[pltpu.VMEM((tm, d), MXU_DTYPE)],
        compiler_params=_params("parallel", "arbitrary"),
        name="ffn_up",
    )(x, g.reshape(1, d), w_gate, w_up)
    return _proj_res(x, h, w_d, scale=0.5)


def _prep_ffn_weights(w_gate_up, w_down):
    f = w_down.shape[0]
    return (w_gate_up[:, :f].astype(MXU_DTYPE), w_gate_up[:, f:].astype(MXU_DTYPE),
            w_down.astype(MXU_DTYPE))


def _norm_mm_kernel(x_ref, g_ref, w_ref, o_ref, xn_ref):
    @pl.when(pl.program_id(1) == 0)
    def _():
        xn_ref[...] = (_rms(x_ref[...]) * g_ref[...]).astype(xn_ref.dtype)

    o_ref[...] = jnp.dot(xn_ref[...], w_ref[...], preferred_element_type=F32)


def _norm_mm(x, g, w):
    t, d = x.shape
    n = w.shape[1]
    tm, tn = _tile(t, 1024), _tile(n, 2048)
    return pl.pallas_call(
        _norm_mm_kernel,
        grid=(t // tm, n // tn),
        in_specs=[
            pl.BlockSpec((tm, d), lambda i, j: (i, 0)),
            pl.BlockSpec((1, d), lambda i, j: (0, 0)),
            pl.BlockSpec((d, tn), lambda i, j: (0, j)),
        ],
        out_specs=pl.BlockSpec((tm, tn), lambda i, j: (i, j)),
        out_shape=jax.ShapeDtypeStruct((t, n), F32),
        scratch_shapes=[pltpu.VMEM((tm, d), MXU_DTYPE)],
        compiler_params=_params("parallel", "arbitrary"),
        name="norm_mm",
    )(x, g.reshape(1, d), w)


def _norm_kernel(x_ref, g_ref, o_ref):
    o_ref[...] = _rms(x_ref[...]) * g_ref[...]


def _norm(x, g):
    t, d = x.shape
    tm = _tile(t, 512)
    return pl.pallas_call(
        _norm_kernel,
        grid=(t // tm,),
        in_specs=[pl.BlockSpec((tm, d), lambda i: (i, 0)), pl.BlockSpec((1, d), lambda i: (0, 0))],
        out_specs=pl.BlockSpec((tm, d), lambda i: (i, 0)),
        out_shape=jax.ShapeDtypeStruct((t, d), F32),
        compiler_params=_params("parallel"),
        name="norm",
    )(x, g.reshape(1, d))


def _proj_res_kernel(x_ref, a_ref, w_ref, o_ref, *, scale):
    y = _mm(a_ref[...], w_ref[...])
    o_ref[...] = x_ref[...] + (y if scale == 1.0 else scale * y)


def _gated_proj_res_kernel(x_ref, a_ref, z_ref, g_ref, w_ref, o_ref, *, sub):
    for r in range(0, x_ref.shape[0], sub):
        rows = slice(r, r + sub)
        an = _rms(a_ref[rows, :]) * g_ref[...] * _silu(z_ref[rows, :])
        o_ref[rows, :] = x_ref[rows, :] + _mm(an, w_ref[...])


def _proj_res(x, a, w, scale=1.0):
    t, d = x.shape
    k = a.shape[1]
    tm = _tile(t, 512)
    return pl.pallas_call(
        functools.partial(_proj_res_kernel, scale=scale),
        grid=(t // tm,),
        in_specs=[
            pl.BlockSpec((tm, d), lambda i: (i, 0)),
            pl.BlockSpec((tm, k), lambda i: (i, 0)),
            pl.BlockSpec((k, d), lambda i: (0, 0), pipeline_mode=pl.Buffered(1)),
        ],
        out_specs=pl.BlockSpec((tm, d), lambda i: (i, 0)),
        out_shape=jax.ShapeDtypeStruct((t, d), F32),
        compiler_params=_params("parallel"),
        name="proj_res",
    )(x, a, w)


def _gated_proj_res(x, a, z_src, z_col, g, w):
    t, d = x.shape
    tm = _tile(t, 512)
    row = lambda col: pl.BlockSpec((tm, d), lambda i: (i, col))
    return pl.pallas_call(
        functools.partial(_gated_proj_res_kernel, sub=_tile(tm, 128)),
        grid=(t // tm,),
        in_specs=[row(0), row(0), row(z_col),
                  pl.BlockSpec((1, d), lambda i: (0, 0)),
                  pl.BlockSpec((d, d), lambda i: (0, 0))],
        out_specs=row(0),
        out_shape=jax.ShapeDtypeStruct((t, d), F32),
        compiler_params=_params("parallel"),
        name="gated_proj_res",
    )(x, a, z_src, g.reshape(1, d), w)


def _hgrn_levels(rows):
    t = np.arange(rows)[:, None]
    r = np.arange(rows)[None, :]
    mats = [r <= t]
    for b in HGRN_FINE_LEVELS:
        mid = (t // b) * b + b // 2 - 1
        upper = (t % b) >= b // 2
        mats.append(np.where(upper, (r > mid) & (r <= t), (r > t) & (r <= mid)))
    return np.concatenate(mats, axis=0).astype(np.float32)


def _hgrn_kernel(q_ref, f_ref, v_ref, lb_ref, lvl_ref, s0_ref, o_ref, sout_ref, st_ref, *, rows, nblk):
    i = pl.program_id(2)

    @pl.when(i == 0)
    def _():
        st_ref[...] = s0_ref[0, 0]

    lb = lb_ref[...]
    lvl = lvl_ref[...]
    t_id = lax.broadcasted_iota(jnp.int32, (rows, HGRN_HEAD), 0)
    ti = lax.broadcasted_iota(jnp.int32, (rows, rows), 0)
    si = lax.broadcasted_iota(jnp.int32, (rows, rows), 1)
    scale = HGRN_HEAD ** -0.5

    def level(scores, b, qe, ke):
        upper = (t_id & (b - 1)) >= b // 2
        s_l = _mm_nt(jnp.where(upper, qe, 0.0), jnp.where(upper, 0.0, ke))
        if b < rows:
            sh = int(math.log2(b))
            s_l = jnp.where((ti >> sh) == (si >> sh), s_l, 0.0)
        return scores + s_l

    def block(sl):
        q = _silu(q_ref[sl, :]) * scale
        fz = f_ref[sl, :]
        v = v_ref[sl, :]
        f = lb + (1.0 - lb) * _sigmoid(fz)
        k = (1.0 - lb) * _sigmoid(-fz)
        sums = _sel_mm(lvl, jnp.log(f))
        yield
        g = sums[:rows]
        scores = jnp.where(ti == si, _mm_nt(q, k), 0.0)
        yield
        scores = level(scores, 2, q * f, k)
        yield
        for n, b in enumerate(HGRN_FINE_LEVELS):
            if b <= rows:
                e = jnp.exp(sums[(n + 1) * rows:(n + 2) * rows])
                scores = level(scores, b, q * e, k * e)
                yield
        b = 2 * HGRN_FINE_LEVELS[0]
        while b <= rows:
            g_mid = jnp.concatenate(
                [jnp.broadcast_to(g[m:m + 1, :], (b, HGRN_HEAD)) for m in range(b // 2 - 1, rows, b)], axis=0)
            upper = (t_id & (b - 1)) >= b // 2
            e = jnp.exp(jnp.where(upper, g - g_mid, g_mid - g))
            scores = level(scores, b, q * e, k * e)
            yield
            b *= 2
        g_last = g[rows - 1:rows, :]
        o_intra = _mm(scores, v)
        yield
        kv = _mm_tn(v, k * jnp.exp(g_last - g))
        return o_intra, q * jnp.exp(g), jnp.exp(g_last), kv

    st = st_ref[...]
    for u0 in range(0, nblk, HGRN_BLOCKS_IN_FLIGHT):
        slices = [slice(u * rows, (u + 1) * rows) for u in range(u0, min(u0 + HGRN_BLOCKS_IN_FLIGHT, nblk))]
        blocks = _lockstep([block(sl) for sl in slices])
        for sl, (o_intra, qg, dec, kv) in zip(slices, blocks):
            o_ref[sl, :] = o_intra + _mm_nt(qg, st)
            st = st * dec + kv
    st_ref[...] = st

    @pl.when(i == pl.num_programs(2) - 1)
    def _():
        sout_ref[0, 0] = st


def _hgrn_recurrence(proj, lb, s0_t, b, t):
    d = proj.shape[1] // 4
    h = d // HGRN_HEAD
    tb = _tile(t, 1024)
    nb = t // tb
    rows = _tile(tb, 128)
    lvl = jnp.asarray(_hgrn_levels(rows), dtype=jnp.bfloat16)
    kern = functools.partial(_hgrn_kernel, rows=rows, nblk=tb // rows)
    row = lambda bi, hi, i: bi * nb + i
    return pl.pallas_call(
        kern,
        grid=(b, h, nb),
        in_specs=[
            pl.BlockSpec((tb, HGRN_HEAD), lambda bi, hi, i: (row(bi, hi, i), hi)),
            pl.BlockSpec((tb, HGRN_HEAD), lambda bi, hi, i: (row(bi, hi, i), h + hi)),
            pl.BlockSpec((tb, HGRN_HEAD), lambda bi, hi, i: (row(bi, hi, i), 2 * h + hi)),
            pl.BlockSpec((1, HGRN_HEAD), lambda bi, hi, i: (0, hi)),
            pl.BlockSpec(lvl.shape, lambda bi, hi, i: (0, 0)),
            pl.BlockSpec((1, 1, HGRN_HEAD, HGRN_HEAD), lambda bi, hi, i: (bi, hi, 0, 0)),
        ],
        out_specs=[
            pl.BlockSpec((tb, HGRN_HEAD), lambda bi, hi, i: (row(bi, hi, i), hi)),
            pl.BlockSpec((1, 1, HGRN_HEAD, HGRN_HEAD), lambda bi, hi, i: (bi, hi, 0, 0)),
        ],
        out_shape=[jax.ShapeDtypeStruct((b * t, d), F32),
                   jax.ShapeDtypeStruct((b, h, HGRN_HEAD, HGRN_HEAD), F32)],
        scratch_shapes=[pltpu.VMEM((HGRN_HEAD, HGRN_HEAD), F32)],
        compiler_params=_params("parallel", "parallel", "arbitrary"),
        name="hgrn_recurrence",
    )(proj, proj, proj, lb.reshape(1, d), lvl, s0_t)


def _softmax_pv(parts):
    m = functools.reduce(jnp.maximum, [jnp.max(s, axis=-1, keepdims=True) for s, _ in parts])
    es = [jnp.exp(s - m) for s, _ in parts]
    den = functools.reduce(jnp.add, [jnp.sum(e, axis=-1, keepdims=True) for e in es])
    yield
    pv = functools.reduce(jnp.add, [_mm(e, v) for e, (_, v) in zip(es, parts)])
    yield
    return pv / den


def _attn_prompt_kernel(q_ref, kp_ref, kc_ref, vp_ref, vc_ref, bias_ref, o_ref, k_scr, v_scr, *, tq):
    i = pl.program_id(2)
    k_scr[0:tq, :] = kp_ref[...].astype(k_scr.dtype)
    k_scr[tq:2 * tq, :] = kc_ref[...].astype(k_scr.dtype)
    v_scr[0:tq, :] = vp_ref[...].astype(v_scr.dtype)
    v_scr[tq:2 * tq, :] = vc_ref[...].astype(v_scr.dtype)
    lane = lax.broadcasted_iota(jnp.int32, (CHUNK, LANES), 1)
    head_a = lane < ATTN_HEAD_DIM
    col = lax.broadcasted_iota(jnp.int32, (CHUNK, BAND), 1)
    scale = ATTN_HEAD_DIM ** -0.5
    past = PREV_CHUNKS * CHUNK

    def unit(j, hd, masked):
        q = q_ref[j * CHUNK:(j + 1) * CHUNK, :] * scale
        w0 = tq - past + j * CHUNK
        sel = head_a if hd == 0 else jnp.logical_not(head_a)
        s = _mm_nt(jnp.where(sel, q, 0.0), k_scr[w0:w0 + BAND, :]) + bias_ref[hd]
        yield
        if masked:
            s = jnp.where(col + ((i - 1) * tq + w0) >= 0, s, -jnp.inf)
        return (yield from _softmax_pv([(s, v_scr[w0:w0 + BAND, :])]))

    def run(masked):
        for j0 in range(0, tq // CHUNK, ATTN_CHUNKS_IN_FLIGHT):
            js = range(j0, min(j0 + ATTN_CHUNKS_IN_FLIGHT, tq // CHUNK))
            outs = _lockstep([unit(j, hd, masked) for j in js for hd in range(2)])
            for n, j in enumerate(js):
                o_ref[j * CHUNK:(j + 1) * CHUNK, :] = jnp.where(head_a, outs[2 * n], outs[2 * n + 1])

    @pl.when(i == 0)
    def _():
        run(True)

    @pl.when(i > 0)
    def _():
        run(False)


def _attn_prompt(qkv, bias, b, t):
    d = qkv.shape[1] // 3
    npair = d // LANES
    tq = _tile(t, 512)
    nq = t // tq
    assert tq >= PREV_CHUNKS * CHUNK and tq % CHUNK == 0
    cur = lambda bi, i: bi * nq + i
    prev = lambda bi, i: bi * nq + jnp.maximum(i - 1, 0)
    return pl.pallas_call(
        functools.partial(_attn_prompt_kernel, tq=tq),
        grid=(b, npair, nq),
        in_specs=[
            pl.BlockSpec((tq, LANES), lambda bi, p, i: (cur(bi, i), p)),
            pl.BlockSpec((tq, LANES), lambda bi, p, i: (prev(bi, i), npair + p)),
            pl.BlockSpec((tq, LANES), lambda bi, p, i: (cur(bi, i), npair + p)),
            pl.BlockSpec((tq, LANES), lambda bi, p, i: (prev(bi, i), 2 * npair + p)),
            pl.BlockSpec((tq, LANES), lambda bi, p, i: (cur(bi, i), 2 * npair + p)),
            pl.BlockSpec((2, CHUNK, BAND), lambda bi, p, i: (p, 0, 0)),
        ],
        out_specs=pl.BlockSpec((tq, LANES), lambda bi, p, i: (cur(bi, i), p)),
        out_shape=jax.ShapeDtypeStruct((b * t, d), F32),
        scratch_shapes=[pltpu.VMEM((2 * tq, LANES), MXU_DTYPE), pltpu.VMEM((2 * tq, LANES), MXU_DTYPE)],
        compiler_params=_params("parallel", "parallel", "arbitrary"),
        name="attn_prompt",
    )(qkv, qkv, qkv, qkv, qkv, bias)


def _attn_cached_kernel(q_ref, kn_ref, vn_ref, kc_ref, vc_ref, bc_ref, bn_ref, o_ref):
    q = q_ref[...]
    lane = lax.broadcasted_iota(jnp.int32, q.shape, 1)
    head_a = lane < ATTN_HEAD_DIM
    scale = ATTN_HEAD_DIM ** -0.5
    kc, vc, kn, vn = kc_ref[0], vc_ref[0], kn_ref[...], vn_ref[...]

    def unit(hd, sel):
        qm = jnp.where(sel, q * scale, 0.0)
        s_c = _mm_nt(qm, kc) + bc_ref[hd]
        s_n = _mm_nt(qm, kn) + bn_ref[hd]
        return (yield from _softmax_pv([(s_c, vc), (s_n, vn)]))

    outs = _lockstep([unit(0, head_a), unit(1, jnp.logical_not(head_a))])
    o_ref[...] = jnp.where(head_a, outs[0], outs[1])


def _attn_cached(qkv, k_cache, v_cache, bias_c, bias_n, b, t):
    d = qkv.shape[1] // 3
    npair = d // LANES
    rows = k_cache.shape[1]
    return pl.pallas_call(
        _attn_cached_kernel,
        grid=(b, npair),
        in_specs=[
            pl.BlockSpec((t, LANES), lambda bi, p: (bi, p)),
            pl.BlockSpec((t, LANES), lambda bi, p: (bi, npair + p)),
            pl.BlockSpec((t, LANES), lambda bi, p: (bi, 2 * npair + p)),
            pl.BlockSpec((1, rows, LANES), lambda bi, p: (bi, 0, p)),
            pl.BlockSpec((1, rows, LANES), lambda bi, p: (bi, 0, p)),
            pl.BlockSpec((2, t, rows), lambda bi, p: (p, 0, 0)),
            pl.BlockSpec((2, t, t), lambda bi, p: (p, 0, 0)),
        ],
        out_specs=pl.BlockSpec((t, LANES), lambda bi, p: (bi, p)),
        out_shape=jax.ShapeDtypeStruct((b * t, d), F32),
        compiler_params=_params("parallel", "parallel"),
        name="attn_cached",
    )(qkv, qkv, qkv, k_cache, v_cache, bias_c, bias_n)


def _rel_bias(rel_bias, nq, nk, offset):
    rel = jnp.arange(nq + nk - 1) + (offset - nk + 1)
    diag = rel_bias[:, jnp.clip(rel, -REL_CLIP, REL_CLIP) + REL_CLIP].astype(F32)
    rev = diag[:, ::-1]
    return jnp.stack([rev[:, nq - 1 - q:nq - 1 - q + nk] for q in range(nq)], axis=1)


def _rwkv_rkv_kernel(h_ref, p_ref, mu_ref, w_ref, o_ref, l_ref):
    @pl.when(pl.program_id(2) == 0)
    def _():
        h = h_ref[...]
        l_ref[...] = (h + (p_ref[...] - h) * mu_ref[0]).astype(l_ref.dtype)

    o_ref[0] = jnp.dot(l_ref[...], w_ref[0], preferred_element_type=F32)


def _rwkv_rkv(h, h_prev, mu, w_rkv):
    t, d = h.shape
    tm, tn = _tile(t, 512), _tile(d, 2048)
    return pl.pallas_call(
        _rwkv_rkv_kernel,
        grid=(3, t // tm, d // tn),
        in_specs=[
            pl.BlockSpec((tm, d), lambda c, i, j: (i, 0)),
            pl.BlockSpec((tm, d), lambda c, i, j: (i, 0)),
            pl.BlockSpec((1, 1, d), lambda c, i, j: (c, 0, 0)),
            pl.BlockSpec((1, d, tn), lambda c, i, j: (c, 0, j)),
        ],
        out_specs=pl.BlockSpec((1, tm, tn), lambda c, i, j: (c, i, j)),
        out_shape=jax.ShapeDtypeStruct((3, t, d), F32),
        scratch_shapes=[pltpu.VMEM((tm, d), MXU_DTYPE)],
        compiler_params=_params("parallel", "parallel", "arbitrary"),
        name="rwkv_rkv",
    )(h, h_prev, mu.reshape(-1, 1, d), w_rkv)


def _softplus(y):
    return jnp.maximum(y, 0.0) + jnp.log1p(jnp.exp(-jnp.abs(y)))


def _rwkv_lora_kernel(h_ref, p_ref, mu_ref, w0_ref, w1_ref, w2_ref, a0_ref, a1_ref, a2_ref,
                      g1_ref, g2_ref, lw_ref, a_ref, gate_ref):
    h = h_ref[...]
    xx = p_ref[...] - h
    lerp = lambda c: h + xx * mu_ref[c]
    z = w0_ref[...] + _mm(jnp.tanh(_mm(lerp(0), w1_ref[...])), w2_ref[...])
    w_log = -_softplus(-z) - 0.5
    lw_ref[...] = -jnp.exp(w_log)
    a_ref[...] = _sigmoid(a0_ref[...] + _mm(_mm(lerp(1), a1_ref[...]), a2_ref[...]))
    gate_ref[...] = _mm(_sigmoid(_mm(lerp(2), g1_ref[...])), g2_ref[...])


def _rwkv_lora(h, h_prev, mu_wag, w0, w1, w2, a0, a1, a2, g1, g2):
    t, d = h.shape
    tm = _tile(t, 256)
    row = pl.BlockSpec((tm, d), lambda i: (i, 0))
    full = lambda a: pl.BlockSpec(a.shape, lambda i: (0,) * a.ndim)
    consts = [mu_wag.reshape(3, 1, d), w0.reshape(1, d), w1, w2, a0.reshape(1, d), a1, a2, g1, g2]
    return pl.pallas_call(
        _rwkv_lora_kernel,
        grid=(t // tm,),
        in_specs=[row, row] + [full(c) for c in consts],
        out_specs=[row, row, row],
        out_shape=[jax.ShapeDtypeStruct((t, d), F32)] * 3,
        compiler_params=_params("parallel"),
        name="rwkv_lora",
    )(h, h_prev, *consts)


def _rwkv_kernel(r_ref, k_ref, v_ref, lw_ref, a_ref, gate_ref, kkw_ref, kaw_ref, rkw_ref,
                 lng_ref, lnb_ref, s0_ref, o_ref, sout_ref, s_ref, *, c_rows, nchunk, npb, ngrp):
    i = pl.program_id(2)
    c2 = 2 * c_rows

    @pl.when(i == 0)
    def _():
        s_ref[...] = s0_ref[...]

    lane = lax.broadcasted_iota(jnp.int32, (c_rows, LANES), 1)
    head_a = lane < RWKV_HEAD
    si = lax.broadcasted_iota(jnp.int32, (c2, c2), 0)
    sj = lax.broadcasted_iota(jnp.int32, (c2, c2), 1)
    same_blk = (si // c_rows) == (sj // c_rows)
    strict = jnp.logical_and(same_blk, (sj % c_rows) < (si % c_rows))
    incl = jnp.logical_and(same_blk, (sj % c_rows) <= (si % c_rows))
    inv_n = 1.0 / RWKV_HEAD

    def stack(x):
        return jnp.concatenate([jnp.where(head_a, x, 0.0), jnp.where(head_a, 0.0, x)], axis=0)

    def head_sum(x):
        sa = jnp.sum(jnp.where(head_a, x, 0.0), axis=-1, keepdims=True)
        sb = jnp.sum(jnp.where(head_a, 0.0, x), axis=-1, keepdims=True)
        return jnp.where(head_a, sa, sb)

    def prepare(rows, lanes):
        r, kr, v = r_ref[rows, lanes], k_ref[rows, lanes], v_ref[rows, lanes]
        lw, a = lw_ref[rows, lanes], a_ref[rows, lanes]
        kkw, kaw = kkw_ref[:, lanes], kaw_ref[:, lanes]
        kk = kr * kkw
        ss = head_sum(kk * kk)
        cl = _cumsum_rows(lw)
        yield
        kk = kk * lax.rsqrt(jnp.maximum(ss, 1e-24))
        k = kr * (1.0 + (a - 1.0) * kaw)
        e_neg = jnp.exp(-cl)
        al = stack(-kk * jnp.exp(cl - lw))
        rt = stack(r * jnp.exp(cl))
        bg = stack(a * kk * e_neg)
        rhs = jnp.concatenate([bg, stack(k * e_neg)], axis=0)
        aa = _mm_nt(jnp.concatenate([al, rt], axis=0), rhs)
        yield
        n_ab = jnp.where(strict, aa[:c2, :c2], 0.0)
        a_ak = jnp.where(strict, aa[:c2, c2:], 0.0)
        a_rb = jnp.where(incl, aa[c2:, :c2], 0.0)
        a_rk = jnp.where(incl, aa[c2:, c2:], 0.0)
        vs = stack(v)
        x = jnp.concatenate([al, _mm(a_ak, vs)], axis=1)
        yield
        pw = n_ab
        x = x + _mm(pw, x)
        yield
        for _ in range(int(math.log2(c_rows)) - 1):
            pw = _mm(pw, pw)
            yield
            x = x + _mm(pw, x)
            yield
        w, u0 = x[:, :LANES], x[:, LANES:]
        y = _mm(a_rb, x)
        yield
        ro = rt + y[:, :LANES]
        o0 = y[:, LANES:] + _mm(a_rk, vs)
        yield
        p = _mm_tn(w, bg)
        yield
        q = _mm_tn(jnp.concatenate([u0, vs], axis=0), rhs)
        yield
        gam = jnp.exp(cl[c_rows - 1:c_rows, :])
        bonus = head_sum(r * k * rkw_ref[:, lanes]) * v
        return ro, o0, p, q, gam, bonus

    def finish(rows, lanes, o2, bonus):
        o = o2[:c_rows] + o2[c_rows:]
        mean = head_sum(o) * inv_n
        dlt = o - mean
        var = head_sum(dlt * dlt) * inv_n
        on =dlt * lax.rsqrt(var + RWKV_GN_EPS) * lng_ref[:, lanes] + lnb_ref[:, lanes]
        o_ref[rows, lanes] = (on + bonus) * gate_ref[rows, lanes]

    def group(g, carry):
        r0 = pl.multiple_of(g * (ngrp * c_rows), ngrp * c_rows)
        lanes = [slice(pi * LANES, (pi + 1) * LANES) for pi in range(npb)]
        rows = [pl.ds(r0 + c * c_rows, c_rows) for c in range(ngrp)]
        units = [(c, pi) for c in range(ngrp) for pi in range(npb)]
        prep = dict(zip(units, _lockstep([prepare(rows[c], lanes[pi]) for c, pi in units])))
        s = [s_ref[pi] for pi in range(npb)]
        for c, pi in units:
            ro, o0, p, q, gam, bonus = prep[c, pi]
            finish(rows[c], lanes[pi], _mm_nt(ro, s[pi]) + o0, bonus)
            s[pi] = (s[pi] + _mm(s[pi], p) + q) * gam
        for pi in range(npb):
            s_ref[pi] = s[pi]
        return carry

    lax.fori_loop(0, nchunk // ngrp, group, 0)

    @pl.when(i == pl.num_programs(2) - 1)
    def _():
        sout_ref[...] = s_ref[...]


def _rwkv_recurrence(rkv, lw, a, gate, kkw, kaw, rkw, lng, lnb, s0_blk, b, t):
    d = lw.shape[1]
    npair = d // LANES
    c_rows = min(CHUNK, t)
    tb = _tile(t, 512)
    nb = t // tb
    nchunk = tb // c_rows
    npb = 2 if npair % 2 == 0 else 1
    ngrp = _tile(nchunk, RWKV_CHUNKS_IN_FLIGHT)
    bw = npb * LANES
    kern = functools.partial(_rwkv_kernel, c_rows=c_rows, nchunk=nchunk, npb=npb, ngrp=ngrp)
    rowblk = pl.BlockSpec((tb, bw), lambda bi, p, i: (bi * nb + i, p))
    rkvblk = lambda c: pl.BlockSpec((None, tb, bw), lambda bi, p, i: (c, bi * nb + i, p))
    vec = pl.BlockSpec((1, bw), lambda bi, p, i: (0, p))
    st = pl.BlockSpec((None, npb, LANES, LANES), lambda bi, p, i: (bi, p, 0, 0))
    return pl.pallas_call(
        kern,
        grid=(b, npair // npb, nb),
        in_specs=[rkvblk(0), rkvblk(1), rkvblk(2), rowblk, rowblk, rowblk, vec, vec, vec, vec, vec, st],
        out_specs=[rowblk, st],
        out_shape=[jax.ShapeDtypeStruct((b * t, d), F32),
                   jax.ShapeDtypeStruct((b, npair, LANES, LANES), F32)],
        scratch_shapes=[pltpu.VMEM((npb, LANES, LANES), F32)],
        compiler_params=_params("parallel", "parallel", "arbitrary"),
        name="rwkv_recurrence",
    )(rkv, rkv, rkv, lw, a, gate, kkw.reshape(1, d), kaw.reshape(1, d), rkw.reshape(1, d),
      lng.reshape(1, d), lnb.reshape(1, d), s0_blk)


def _to_blockdiag(s):
    b, h, n, _ = s.shape
    s = s.reshape(b, h // 2, 2, n, n)
    z = jnp.zeros_like(s[:, :, 0])
    top = jnp.concatenate([s[:, :, 0], z], axis=-1)
    bot = jnp.concatenate([z, s[:, :, 1]], axis=-1)
    return jnp.concatenate([top, bot], axis=-2)


def _from_blockdiag(sb):
    b, p, n2, _ = sb.shape
    n = n2 // 2
    return jnp.stack([sb[:, :, :n, :n], sb[:, :, n:, n:]], axis=2).reshape(b, 2 * p, n, n)


def _trunk(x3, hgrn_s0, k_cache, v_cache, wkv_s0, shift0, w, lower):
    b, t, d = x3.shape
    x = x3.reshape(b * t, d)
    depth = w['norm_g'].shape[0]
    hgrn_out, k_out, v_out, wkv_out, shift_out = [], [], [], [], []
    for layer in range(depth):
        kind, j = layer % 3, layer // 3
        g = w['norm_g'][layer]
        x = _ffn(x, g[0], *w['ffn'][layer][0])
        if kind == 0:
            proj = _norm_mm(x, g[1], w['hgrn_w_in'][j])
            s0_t = jnp.swapaxes(hgrn_s0[j], -1, -2)
            o, s_t = _hgrn_recurrence(proj, lower[layer], s0_t, b, t)
            hgrn_out.append(jnp.swapaxes(s_t, -1, -2))
            x = _gated_proj_res(x, o, proj, 3, w['hgrn_norm_g'][j], w['hgrn_w_out'][j])
        elif kind == 1:
            qkv = _norm_mm(x, g[1], w['attn_w_qkv'][j])
            rel_bias = w['attn_rel_bias'][j]
            nh = rel_bias.shape[0]
            if k_cache is None:
                past = PREV_CHUNKS * CHUNK
                o = _attn_prompt(qkv, _rel_bias(rel_bias, CHUNK, BAND, past), b, t)
                kept = min(past, t)
            else:
                rows = k_cache.shape[2]
                bias = _rel_bias(rel_bias, t, rows + t, rows)
                o = _attn_cached(qkv, k_cache[j].reshape(b, rows, d), v_cache[j].reshape(b, rows, d),
                                 bias[:, :, :rows], bias[:, :, rows:], b, t)
                kept = t
            qkv3 = qkv.reshape(b, t, 3 * d)
            keep = lambda c: qkv3[:, t - kept:, c * d:(c + 1) * d].reshape(b, kept, nh, ATTN_HEAD_DIM)
            k_out.append(keep(1))
            v_out.append(keep(2))
            x = _proj_res(x, o, w['attn_w_out'][j])
        else:
            h = _norm(x, g[1])
            h3 = h.reshape(b, t, d)
            h_prev = jnp.concatenate([shift0[j].astype(F32), h3[:, :-1]], axis=1).reshape(b * t, d)
            mu = w['rwkv_mu'][j]
            rkv = _rwkv_rkv(h, h_prev, mu[:3], w['rwkv_w_rkv'][j])
            lw, a, gate = _rwkv_lora(h, h_prev, mu[3:], w['rwkv_w0'][j], w['rwkv_w1'][j], w['rwkv_w2'][j],
                                     w['rwkv_a0'][j], w['rwkv_a1'][j], w['rwkv_a2'][j],
                                     w['rwkv_g1'][j], w['rwkv_g2'][j])
            o, s_blk = _rwkv_recurrence(rkv, lw, a, gate, w['rwkv_k_k'][j], w['rwkv_k_a'][j],
                                        w['rwkv_r_k'][j].reshape(-1), w['rwkv_ln_g'][j], w['rwkv_ln_b'][j],
                                        _to_blockdiag(wkv_s0[j]), b, t)
            shift_out.append(h3[:, -1:])
            wkv_out.append(_from_blockdiag(s_blk))
            x = _proj_res(x, o, w['rwkv_w_out'][j])
        x = _ffn(x, g[2], *w['ffn'][layer][1])
    y = _norm(x, w['final_norm_g']).reshape(b, t, d)
    return (y, jnp.stack(hgrn_out), jnp.stack(k_out), jnp.stack(v_out),
            jnp.stack(wkv_out), jnp.stack(shift_out))


def kernel(x_prompt, x_sample, state_hgrn, cache_k_band, cache_v_band, state_wkv, state_shift,
           norm_g, final_norm_g, ffn_w_gate_up, ffn_w_down,
           hgrn_w_in, hgrn_lb_logits, hgrn_norm_g, hgrn_w_out,
           attn_w_qkv, attn_rel_bias, attn_w_out,
           rwkv_mu, rwkv_w_rkv, rwkv_w0, rwkv_w1, rwkv_w2, rwkv_a0, rwkv_a1, rwkv_a2,
           rwkv_g1, rwkv_g2, rwkv_k_k, rwkv_k_a, rwkv_r_k, rwkv_ln_g, rwkv_ln_b, rwkv_w_out):
    depth = norm_g.shape[0]
    cast = lambda a: a.astype(MXU_DTYPE)
    ffn = [[_prep_ffn_weights(ffn_w_gate_up[l, i], ffn_w_down[l, i]) for i in range(2)] for l in range(depth)]
    w = dict(norm_g=norm_g, final_norm_g=final_norm_g, ffn=ffn,
             hgrn_w_in=cast(hgrn_w_in), hgrn_norm_g=hgrn_norm_g, hgrn_w_out=cast(hgrn_w_out),
             attn_w_qkv=cast(attn_w_qkv), attn_rel_bias=attn_rel_bias, attn_w_out=cast(attn_w_out),
             rwkv_mu=rwkv_mu, rwkv_w_rkv=cast(rwkv_w_rkv), rwkv_w0=rwkv_w0, rwkv_w1=cast(rwkv_w1),
             rwkv_w2=cast(rwkv_w2), rwkv_a0=rwkv_a0, rwkv_a1=cast(rwkv_a1), rwkv_a2=cast(rwkv_a2),
             rwkv_g1=cast(rwkv_g1), rwkv_g2=cast(rwkv_g2), rwkv_k_k=rwkv_k_k, rwkv_k_a=rwkv_k_a,
             rwkv_r_k=rwkv_r_k, rwkv_ln_g=rwkv_ln_g, rwkv_ln_b=rwkv_ln_b, rwkv_w_out=cast(rwkv_w_out))
    probs = jax.nn.softmax(hgrn_lb_logits.astype(F32), axis=0)
    lower = jnp.cumsum(probs, axis=0) - probs[0]

    b = x_prompt.shape[0]
    d = x_prompt.shape[2]
    n_a, n_c = state_hgrn.shape[0], state_wkv.shape[0]
    hgrn0 = jnp.zeros((n_a, b) + state_hgrn.shape[2:], F32)
    wkv0 = jnp.zeros((n_c, b) + state_wkv.shape[2:], F32)
    shift0 = jnp.zeros((n_c, b, 1, d), F32)
    y_p, hgrn_p, k_p, v_p, wkv_p, shift_p = _trunk(x_prompt, hgrn0, None, None, wkv0, shift0, w, lower)
    y_s, hgrn_s, k_s, v_s, wkv_s, shift_s = _trunk(x_sample, state_hgrn, cache_k_band, cache_v_band,
                                                   state_wkv, state_shift, w, lower)
    return (y_p, y_s, hgrn_p, hgrn_s, k_p, v_p, k_s, v_s, wkv_p, wkv_s, shift_p, shift_s)
```

```python
import functools
import math

import numpy as np
import jax
import jax.numpy as jnp
from jax import lax
from jax.experimental import pallas as pl
from jax.experimental.pallas import tpu as pltpu

F32 = jnp.float32
MXU_DTYPE = jnp.bfloat16

LANES = 128
CHUNK = 64
PREV_CHUNKS = 8
BAND = (PREV_CHUNKS + 1) * CHUNK
REL_CLIP = 128
NORM_EPS = 1e-6
RWKV_GN_EPS = 64e-5
HGRN_HEAD = 128
HGRN_BLOCKS_IN_FLIGHT = 4
HGRN_FINE_LEVELS = (8, 4)
ATTN_HEAD_DIM = 64
ATTN_CHUNKS_IN_FLIGHT = 4
RWKV_HEAD = 64
RWKV_CHUNKS_IN_FLIGHT = 8
VMEM_LIMIT = 60 * 1024 * 1024


def _params(*sem):
    return pltpu.CompilerParams(dimension_semantics=sem, vmem_limit_bytes=VMEM_LIMIT)


def _tile(n, pref):
    if n <= pref:
        return n
    t = pref
    while n % t:
        t //= 2
    return t


def _mm(a, b):
    return jnp.dot(a.astype(MXU_DTYPE), b.astype(MXU_DTYPE), preferred_element_type=F32)


def _mm_nt(a, b):
    return lax.dot_general(a.astype(MXU_DTYPE), b.astype(MXU_DTYPE),
                           (((1,), (1,)), ((), ())), preferred_element_type=F32)


def _mm_tn(a, b):
    return lax.dot_general(a.astype(MXU_DTYPE), b.astype(MXU_DTYPE),
                           (((0,), (0,)), ((), ())), preferred_element_type=F32)


def _split3(x):
    hi = x.astype(jnp.bfloat16)
    r1 = x - hi.astype(F32)
    mid = r1.astype(jnp.bfloat16)
    lo = (r1 - mid.astype(F32)).astype(jnp.bfloat16)
    return hi, mid, lo


def _sel_mm(sel, x):
    sel = sel.astype(jnp.bfloat16)
    hi, mid, lo = _split3(x)
    d = lambda p: jnp.dot(sel, p, preferred_element_type=F32)
    return d(hi) + (d(mid) + d(lo))


def _cumsum_rows(x):
    row = lax.broadcasted_iota(jnp.int32, x.shape, 0)
    shift = 1
    while shift < x.shape[0]:
        x = x + jnp.where(row >= shift, pltpu.roll(x, shift, axis=0), 0.0)
        shift *= 2
    return x


def _lockstep(gens):
    out = [None] * len(gens)
    live = list(range(len(gens)))
    while live:
        for n in list(live):
            try:
                next(gens[n])
            except StopIteration as stop:
                out[n] = stop.value
                live.remove(n)
    return out


def _rms(x):
    return x * lax.rsqrt(jnp.mean(x * x, axis=-1, keepdims=True) + NORM_EPS)


def _sigmoid(x):
    return 1.0 / (1.0 + jnp.exp(-x))


def _silu(x):
    return x * _sigmoid(x)


def _ffn_up_kernel(xn_ref, wg_ref, wu_ref, h_ref, *, tail):
    j = pl.program_id(1)

    def columns(width):
        xn = xn_ref[...]
        gate = jnp.dot(xn, wg_ref[:, :width], preferred_element_type=F32)
        up = jnp.dot(xn, wu_ref[:, :width], preferred_element_type=F32)
        h_ref[:, :width] = (_silu(gate) * up).astype(h_ref.dtype)

    last = pl.num_programs(1) - 1
    pl.when(j < last)(lambda: columns(h_ref.shape[1]))
    pl.when(j == last)(lambda: columns(tail))


def _ffn(x, xn, w_gate, w_up, w_d, g_next, n_dtype, keep_x=True):
    t, d = x.shape
    f = w_d.shape[0]
    tf = min(f, 512)
    nf = pl.cdiv(f, tf)
    tm = _tile(t, 2048)
    h = pl.pallas_call(
        functools.partial(_ffn_up_kernel, tail=f - (nf - 1) * tf),
        grid=(t // tm, nf),
        in_specs=[
            pl.BlockSpec((tm, d), lambda i, j: (i, 0)),
            pl.BlockSpec((d, tf), lambda i, j: (0, j)),
            pl.BlockSpec((d, tf), lambda i, j: (0, j)),
        ],
        out_specs=pl.BlockSpec((tm, tf), lambda i, j: (i, j)),
        out_shape=jax.ShapeDtypeStruct((t, f), MXU_DTYPE),
        compiler_params=_params("parallel", "parallel"),
        name="ffn_up",
    )(xn, w_gate, w_up)
    return _proj_res(x, h, w_d, g_next, n_dtype, scale=0.5, keep_x=keep_x)


def _prep_ffn_weights(w_gate_up, w_down):
    f = w_down.shape[0]
    return (w_gate_up[:, :f].astype(MXU_DTYPE), w_gate_up[:, f:].astype(MXU_DTYPE),
            w_down.astype(MXU_DTYPE))


def _proj_kernel(xn_ref, w_ref, o_ref):
    o_ref[...] = jnp.dot(xn_ref[...], w_ref[...], preferred_element_type=F32)


def _proj(xn, w):
    t, d = xn.shape
    n = w.shape[1]
    tm, tn = _tile(t, 1024), _tile(n, 2048)
    return pl.pallas_call(
        _proj_kernel,
        grid=(t // tm, n // tn),
        in_specs=[
            pl.BlockSpec((tm, d), lambda i, j: (i, 0)),
            pl.BlockSpec((d, tn), lambda i, j: (0, j)),
        ],
        out_specs=pl.BlockSpec((tm, tn), lambda i, j: (i, j)),
        out_shape=jax.ShapeDtypeStruct((t, n), F32),
        compiler_params=_params("parallel", "parallel"),
        name="proj",
    )(xn, w)


def _norm_kernel(x_ref, g_ref, o_ref):
    o_ref[...] = (_rms(x_ref[...]) * g_ref[...]).astype(o_ref.dtype)


def _norm(x, g, dtype):
    t, d = x.shape
    tm = _tile(t, 512)
    return pl.pallas_call(
        _norm_kernel,
        grid=(t // tm,),
        in_specs=[pl.BlockSpec((tm, d), lambda i: (i, 0)), pl.BlockSpec((1, d), lambda i: (0, 0))],
        out_specs=pl.BlockSpec((tm, d), lambda i: (i, 0)),
        out_shape=jax.ShapeDtypeStruct((t, d), dtype),
        compiler_params=_params("parallel"),
        name="norm",
    )(x, g.reshape(1, d))


def _emit(y, gn_ref, o_ref, n_ref, rows=slice(None)):
    if o_ref is not None:
        o_ref[rows, :] = y
    n_ref[rows, :] = (_rms(y) * gn_ref[...]).astype(n_ref.dtype)


def _proj_res_kernel(x_ref, a_ref, w_ref, gn_ref, *out_refs, scale):
    o_ref, n_ref = out_refs if len(out_refs) == 2 else (None, out_refs[0])
    y = _mm(a_ref[...], w_ref[...])
    _emit(x_ref[...] + (y if scale == 1.0 else scale * y), gn_ref, o_ref, n_ref)


def _gated_proj_res_kernel(x_ref, a_ref, z_ref, g_ref, w_ref, gn_ref, o_ref, n_ref, *, sub):
    for r in range(0, x_ref.shape[0], sub):
        rows = slice(r, r + sub)
        an = _rms(a_ref[rows, :]) * g_ref[...] * _silu(z_ref[rows, :])
        _emit(x_ref[rows, :] + _mm(an, w_ref[...]), gn_ref, o_ref, n_ref, rows)


def _resident(shape):
    return pl.BlockSpec(shape, lambda i: (0,) * len(shape), pipeline_mode=pl.Buffered(1))


def _proj_res(x, a, w, g_next, n_dtype, scale=1.0, keep_x=True):
    t, d = x.shape
    k = a.shape[1]
    tm = _tile(t, 512)
    row = pl.BlockSpec((tm, d), lambda i: (i, 0))
    outs = pl.pallas_call(
        functools.partial(_proj_res_kernel, scale=scale),
        grid=(t // tm,),
        in_specs=[row, pl.BlockSpec((tm, k), lambda i: (i, 0)), _resident((k, d)), _resident((1, d))],
        out_specs=[row] * (1 + keep_x),
        out_shape=[jax.ShapeDtypeStruct((t, d), F32)] * keep_x + [jax.ShapeDtypeStruct((t, d), n_dtype)],
        compiler_params=_params("parallel"),
        name="proj_res",
    )(x, a, w, g_next.reshape(1, d))
    return tuple(outs) if keep_x else (None, outs[0])


def _gated_proj_res(x, a, z_src, z_col, g, w, g_next, n_dtype):
    t, d = x.shape
    tm = _tile(t, 512)
    row = lambda col: pl.BlockSpec((tm, d), lambda i: (i, col))
    return pl.pallas_call(
        functools.partial(_gated_proj_res_kernel, sub=_tile(tm, 128)),
        grid=(t // tm,),
        in_specs=[row(0), row(0), row(z_col), _resident((1, d)), _resident((d, d)), _resident((1, d))],
        out_specs=[row(0), row(0)],
        out_shape=[jax.ShapeDtypeStruct((t, d), F32), jax.ShapeDtypeStruct((t, d), n_dtype)],
        compiler_params=_params("parallel"),
        name="gated_proj_res",
    )(x, a, z_src, g.reshape(1, d), w, g_next.reshape(1, d))


def _hgrn_levels(rows):
    t = np.arange(rows)[:, None]
    r = np.arange(rows)[None, :]
    mats = [r <= t]
    for b in HGRN_FINE_LEVELS:
        mid = (t // b) * b + b // 2 - 1
        upper = (t % b) >= b // 2
        mats.append(np.where(upper, (r > mid) & (r <= t), (r > t) & (r <= mid)))
    return np.concatenate(mats, axis=0).astype(np.float32)


def _hgrn_kernel(q_ref, f_ref, v_ref, lb_ref, lvl_ref, s0_ref, o_ref, sout_ref, st_ref, *, rows, nblk):
    i = pl.program_id(2)

    @pl.when(i == 0)
    def _():
        st_ref[...] = s0_ref[0, 0]

    lb = lb_ref[...]
    lvl = lvl_ref[...]
    t_id = lax.broadcasted_iota(jnp.int32, (rows, HGRN_HEAD), 0)
    ti = lax.broadcasted_iota(jnp.int32, (rows, rows), 0)
    si = lax.broadcasted_iota(jnp.int32, (rows, rows), 1)
    scale = HGRN_HEAD ** -0.5

    def level(scores, b, qe, ke):
        upper = (t_id & (b - 1)) >= b // 2
        s_l = _mm_nt(jnp.where(upper, qe, 0.0), jnp.where(upper, 0.0, ke))
        if b < rows:
            sh = int(math.log2(b))
            s_l = jnp.where((ti >> sh) == (si >> sh), s_l, 0.0)
        return scores + s_l

    def block(sl):
        q = _silu(q_ref[sl, :]) * scale
        fz = f_ref[sl, :]
        v = v_ref[sl, :]
        f = lb + (1.0 - lb) * _sigmoid(fz)
        k = (1.0 - lb) * _sigmoid(-fz)
        sums = _sel_mm(lvl, jnp.log(f))
        yield
        g = sums[:rows]
        scores = jnp.where(ti == si, _mm_nt(q, k), 0.0)
        yield
        scores = level(scores, 2, q * f, k)
        yield
        for n, b in enumerate(HGRN_FINE_LEVELS):
            if b <= rows:
                e = jnp.exp(sums[(n + 1) * rows:(n + 2) * rows])
                scores = level(scores, b, q * e, k * e)
                yield
        b = 2 * HGRN_FINE_LEVELS[0]
        while b <= rows:
            g_mid = jnp.concatenate(
                [jnp.broadcast_to(g[m:m + 1, :], (b, HGRN_HEAD)) for m in range(b // 2 - 1, rows, b)], axis=0)
            upper = (t_id & (b - 1)) >= b // 2
            e = jnp.exp(jnp.where(upper, g - g_mid, g_mid - g))
            scores = level(scores, b, q * e, k * e)
            yield
            b *= 2
        g_last = g[rows - 1:rows, :]
        o_intra = _mm(scores, v)
        yield
        kv = _mm_tn(v, k * jnp.exp(g_last - g))
        return o_intra, q * jnp.exp(g), jnp.exp(g_last), kv

    st = st_ref[...]
    for u0 in range(0, nblk, HGRN_BLOCKS_IN_FLIGHT):
        slices = [slice(u * rows, (u + 1) * rows) for u in range(u0, min(u0 + HGRN_BLOCKS_IN_FLIGHT, nblk))]
        blocks = _lockstep([block(sl) for sl in slices])
        for sl, (o_intra, qg, dec, kv) in zip(slices, blocks):
            o_ref[sl, :] = o_intra + _mm_nt(qg, st)
            st = st * dec + kv
    st_ref[...] = st

    @pl.when(i == pl.num_programs(2) - 1)
    def _():
        sout_ref[0, 0] = st


def _hgrn_recurrence(proj, lb, s0_t, b, t):
    d = proj.shape[1] // 4
    h = d // HGRN_HEAD
    tb = _tile(t, 1024)
    nb = t // tb
    rows = _tile(tb, 128)
    lvl = jnp.asarray(_hgrn_levels(rows), dtype=jnp.bfloat16)
    kern = functools.partial(_hgrn_kernel, rows=rows, nblk=tb // rows)
    row = lambda bi, hi, i: bi * nb + i
    return pl.pallas_call(
        kern,
        grid=(b, h, nb),
        in_specs=[
            pl.BlockSpec((tb, HGRN_HEAD), lambda bi, hi, i: (row(bi, hi, i), hi)),
            pl.BlockSpec((tb, HGRN_HEAD), lambda bi, hi, i: (row(bi, hi, i), h + hi)),
            pl.BlockSpec((tb, HGRN_HEAD), lambda bi, hi, i: (row(bi, hi, i), 2 * h + hi)),
            pl.BlockSpec((1, HGRN_HEAD), lambda bi, hi, i: (0, hi)),
            pl.BlockSpec(lvl.shape, lambda bi, hi, i: (0, 0)),
            pl.BlockSpec((1, 1, HGRN_HEAD, HGRN_HEAD), lambda bi, hi, i: (bi, hi, 0, 0)),
        ],
        out_specs=[
            pl.BlockSpec((tb, HGRN_HEAD), lambda bi, hi, i: (row(bi, hi, i), hi)),
            pl.BlockSpec((1, 1, HGRN_HEAD, HGRN_HEAD), lambda bi, hi, i: (bi, hi, 0, 0)),
        ],
        out_shape=[jax.ShapeDtypeStruct((b * t, d), F32),
                   jax.ShapeDtypeStruct((b, h, HGRN_HEAD, HGRN_HEAD), F32)],
        scratch_shapes=[pltpu.VMEM((HGRN_HEAD, HGRN_HEAD), F32)],
        compiler_params=_params("parallel", "parallel", "arbitrary"),
        name="hgrn_recurrence",
    )(proj, proj, proj, lb.reshape(1, d), lvl, s0_t)


def _softmax_pv(parts):
    m = functools.reduce(jnp.maximum, [jnp.max(s, axis=-1, keepdims=True) for s, _ in parts])
    es = [jnp.exp(s - m) for s, _ in parts]
    den = functools.reduce(jnp.add, [jnp.sum(e, axis=-1, keepdims=True) for e in es])
    yield
    pv = functools.reduce(jnp.add, [_mm(e, v) for e, (_, v) in zip(es, parts)])
    yield
    return pv / den


def _attn_prompt_kernel(q_ref, kp_ref, kc_ref, vp_ref, vc_ref, bias_ref, o_ref, k_scr, v_scr, *, tq):
    i = pl.program_id(2)
    k_scr[0:tq, :] = kp_ref[...].astype(k_scr.dtype)
    k_scr[tq:2 * tq, :] = kc_ref[...].astype(k_scr.dtype)
    v_scr[0:tq, :] = vp_ref[...].astype(v_scr.dtype)
    v_scr[tq:2 * tq, :] = vc_ref[...].astype(v_scr.dtype)
    lane = lax.broadcasted_iota(jnp.int32, (CHUNK, LANES), 1)
    head_a = lane < ATTN_HEAD_DIM
    col = lax.broadcasted_iota(jnp.int32, (CHUNK, BAND), 1)
    scale = ATTN_HEAD_DIM ** -0.5
    past = PREV_CHUNKS * CHUNK

    def unit(j, hd, masked):
        q = q_ref[j * CHUNK:(j + 1) * CHUNK, :] * scale
        w0 = tq - past + j * CHUNK
        sel = head_a if hd == 0 else jnp.logical_not(head_a)
        s = _mm_nt(jnp.where(sel, q, 0.0), k_scr[w0:w0 + BAND, :]) + bias_ref[hd]
        yield
        if masked:
            s = jnp.where(col + ((i - 1) * tq + w0) >= 0, s, -jnp.inf)
        return (yield from _softmax_pv([(s, v_scr[w0:w0 + BAND, :])]))

    def run(masked):
        for j0 in range(0, tq // CHUNK, ATTN_CHUNKS_IN_FLIGHT):
            js = range(j0, min(j0 + ATTN_CHUNKS_IN_FLIGHT, tq // CHUNK))
            outs = _lockstep([unit(j, hd, masked) for j in js for hd in range(2)])
            for n, j in enumerate(js):
                o_ref[j * CHUNK:(j + 1) * CHUNK, :] = jnp.where(head_a, outs[2 * n], outs[2 * n + 1])

    @pl.when(i == 0)
    def _():
        run(True)

    @pl.when(i > 0)
    def _():
        run(False)


def _attn_prompt(qkv, bias, b, t):
    d = qkv.shape[1] // 3
    npair = d // LANES
    tq = _tile(t, 512)
    nq = t // tq
    assert tq >= PREV_CHUNKS * CHUNK and tq % CHUNK == 0
    cur = lambda bi, i: bi * nq + i
    prev = lambda bi, i: bi * nq + jnp.maximum(i - 1, 0)
    return pl.pallas_call(
        functools.partial(_attn_prompt_kernel, tq=tq),
        grid=(b, npair, nq),
        in_specs=[
            pl.BlockSpec((tq, LANES), lambda bi, p, i: (cur(bi, i), p)),
            pl.BlockSpec((tq, LANES), lambda bi, p, i: (prev(bi, i), npair + p)),
            pl.BlockSpec((tq, LANES), lambda bi, p, i: (cur(bi, i), npair + p)),
            pl.BlockSpec((tq, LANES), lambda bi, p, i: (prev(bi, i), 2 * npair + p)),
            pl.BlockSpec((tq, LANES), lambda bi, p, i: (cur(bi, i), 2 * npair + p)),
            pl.BlockSpec((2, CHUNK, BAND), lambda bi, p, i: (p, 0, 0)),
        ],
        out_specs=pl.BlockSpec((tq, LANES), lambda bi, p, i: (cur(bi, i), p)),
        out_shape=jax.ShapeDtypeStruct((b * t, d), F32),
        scratch_shapes=[pltpu.VMEM((2 * tq, LANES), MXU_DTYPE), pltpu.VMEM((2 * tq, LANES), MXU_DTYPE)],
        compiler_params=_params("parallel", "parallel", "arbitrary"),
        name="attn_prompt",
    )(qkv, qkv, qkv, qkv, qkv, bias)


def _attn_cached_kernel(q_ref, kn_ref, vn_ref, kc_ref, vc_ref, bc_ref, bn_ref, o_ref):
    q = q_ref[...]
    lane = lax.broadcasted_iota(jnp.int32, q.shape, 1)
    head_a = lane < ATTN_HEAD_DIM
    scale = ATTN_HEAD_DIM ** -0.5
    kc, vc, kn, vn = kc_ref[0], vc_ref[0], kn_ref[...], vn_ref[...]

    def unit(hd, sel):
        qm = jnp.where(sel, q * scale, 0.0)
        s_c = _mm_nt(qm, kc) + bc_ref[hd]
        s_n = _mm_nt(qm, kn) + bn_ref[hd]
        return (yield from _softmax_pv([(s_c, vc), (s_n, vn)]))

    outs = _lockstep([unit(0, head_a), unit(1, jnp.logical_not(head_a))])
    o_ref[...] = jnp.where(head_a, outs[0], outs[1])


def _attn_cached(qkv, k_cache, v_cache, bias_c, bias_n, b, t):
    d = qkv.shape[1] // 3
    npair = d // LANES
    rows = k_cache.shape[1]
    return pl.pallas_call(
        _attn_cached_kernel,
        grid=(b, npair),
        in_specs=[
            pl.BlockSpec((t, LANES), lambda bi, p: (bi, p)),
            pl.BlockSpec((t, LANES), lambda bi, p: (bi, npair + p)),
            pl.BlockSpec((t, LANES), lambda bi, p: (bi, 2 * npair + p)),
            pl.BlockSpec((1, rows, LANES), lambda bi, p: (bi, 0, p)),
            pl.BlockSpec((1, rows, LANES), lambda bi, p: (bi, 0, p)),
            pl.BlockSpec((2, t, rows), lambda bi, p: (p, 0, 0)),
            pl.BlockSpec((2, t, t), lambda bi, p: (p, 0, 0)),
        ],
        out_specs=pl.BlockSpec((t, LANES), lambda bi, p: (bi, p)),
        out_shape=jax.ShapeDtypeStruct((b * t, d), F32),
        compiler_params=_params("parallel", "parallel"),
        name="attn_cached",
    )(qkv, qkv, qkv, k_cache, v_cache, bias_c, bias_n)


def _rel_bias(rel_bias, nq, nk, offset):
    rel = jnp.arange(nq + nk - 1) + (offset - nk + 1)
    diag = rel_bias[:, jnp.clip(rel, -REL_CLIP, REL_CLIP) + REL_CLIP].astype(F32)
    rev = diag[:, ::-1]
    return jnp.stack([rev[:, nq - 1 - q:nq - 1 - q + nk] for q in range(nq)], axis=1)


def _rwkv_rkv_kernel(h_ref, p_ref, mu_ref, w_ref, o_ref, l_ref):
    @pl.when(pl.program_id(2) == 0)
    def _():
        h = h_ref[...]
        l_ref[...] = (h + (p_ref[...] - h) * mu_ref[0]).astype(l_ref.dtype)

    o_ref[0] = jnp.dot(l_ref[...], w_ref[0], preferred_element_type=F32)


def _rwkv_rkv(h, h_prev, mu, w_rkv):
    t, d = h.shape
    tm, tn = _tile(t, 512), _tile(d, 2048)
    return pl.pallas_call(
        _rwkv_rkv_kernel,
        grid=(3, t // tm, d // tn),
        in_specs=[
            pl.BlockSpec((tm, d), lambda c, i, j: (i, 0)),
            pl.BlockSpec((tm, d), lambda c, i, j: (i, 0)),
            pl.BlockSpec((1, 1, d), lambda c, i, j: (c, 0, 0)),
            pl.BlockSpec((1, d, tn), lambda c, i, j: (c, 0, j)),
        ],
        out_specs=pl.BlockSpec((1, tm, tn), lambda c, i, j: (c, i, j)),
        out_shape=jax.ShapeDtypeStruct((3, t, d), F32),
        scratch_shapes=[pltpu.VMEM((tm, d), MXU_DTYPE)],
        compiler_params=_params("parallel", "parallel", "arbitrary"),
        name="rwkv_rkv",
    )(h, h_prev, mu.reshape(-1, 1, d), w_rkv)


def _softplus(y):
    return jnp.maximum(y, 0.0) + jnp.log1p(jnp.exp(-jnp.abs(y)))


def _rwkv_lora_kernel(h_ref, p_ref, mu_ref, w0_ref, w1_ref, w2_ref, a0_ref, a1_ref, a2_ref,
                      g1_ref, g2_ref, lw_ref, a_ref, gate_ref):
    h = h_ref[...]
    xx = p_ref[...] - h
    lerp = lambda c: h + xx * mu_ref[c]
    z = w0_ref[...] + _mm(jnp.tanh(_mm(lerp(0), w1_ref[...])), w2_ref[...])
    w_log = -_softplus(-z) - 0.5
    lw_ref[...] = -jnp.exp(w_log)
    a_ref[...] = _sigmoid(a0_ref[...] + _mm(_mm(lerp(1), a1_ref[...]), a2_ref[...]))
    gate_ref[...] = _mm(_sigmoid(_mm(lerp(2), g1_ref[...])), g2_ref[...])


def _rwkv_lora(h, h_prev, mu_wag, w0, w1, w2, a0, a1, a2, g1, g2):
    t, d = h.shape
    tm = _tile(t, 256)
    row = pl.BlockSpec((tm, d), lambda i: (i, 0))
    full = lambda a: pl.BlockSpec(a.shape, lambda i: (0,) * a.ndim)
    consts = [mu_wag.reshape(3, 1, d), w0.reshape(1, d), w1, w2, a0.reshape(1, d), a1, a2, g1, g2]
    return pl.pallas_call(
        _rwkv_lora_kernel,
        grid=(t // tm,),
        in_specs=[row, row] + [full(c) for c in consts],
        out_specs=[row, row, row],
        out_shape=[jax.ShapeDtypeStruct((t, d), F32)] * 3,
        compiler_params=_params("parallel"),
        name="rwkv_lora",
    )(h, h_prev, *consts)


def _rwkv_kernel(r_ref, k_ref, v_ref, lw_ref, a_ref, gate_ref, kkw_ref, kaw_ref, rkw_ref,
                 lng_ref, lnb_ref, s0_ref, o_ref, sout_ref, s_ref, *, c_rows, nchunk, npb, ngrp):
    i = pl.program_id(2)
    c2 = 2 * c_rows

    @pl.when(i == 0)
    def _():
        s_ref[...] = s0_ref[...]

    lane = lax.broadcasted_iota(jnp.int32, (c_rows, LANES), 1)
    head_a = lane < RWKV_HEAD
    si = lax.broadcasted_iota(jnp.int32, (c2, c2), 0)
    sj = lax.broadcasted_iota(jnp.int32, (c2, c2), 1)
    same_blk = (si // c_rows) == (sj // c_rows)
    strict = jnp.logical_and(same_blk, (sj % c_rows) < (si % c_rows))
    incl = jnp.logical_and(same_blk, (sj % c_rows) <= (si % c_rows))
    inv_n = 1.0 / RWKV_HEAD

    def stack(x):
        return jnp.concatenate([jnp.where(head_a, x, 0.0), jnp.where(head_a, 0.0, x)], axis=0)

    def head_sum(x):
        sa = jnp.sum(jnp.where(head_a, x, 0.0), axis=-1, keepdims=True)
        sb = jnp.sum(jnp.where(head_a, 0.0, x), axis=-1, keepdims=True)
        return jnp.where(head_a, sa, sb)

    def prepare(rows, lanes):
        r, kr, v = r_ref[rows, lanes], k_ref[rows, lanes], v_ref[rows, lanes]
        lw, a = lw_ref[rows, lanes], a_ref[rows, lanes]
        kkw, kaw = kkw_ref[:, lanes], kaw_ref[:, lanes]
        kk = kr * kkw
        ss = head_sum(kk * kk)
        cl = _cumsum_rows(lw)
        yield
        kk = kk * lax.rsqrt(jnp.maximum(ss, 1e-24))
        k = kr * (1.0 + (a - 1.0) * kaw)
        e_neg = jnp.exp(-cl)
        al = stack(-kk * jnp.exp(cl - lw))
        rt = stack(r * jnp.exp(cl))
        bg = stack(a * kk * e_neg)
        rhs = jnp.concatenate([bg, stack(k * e_neg)], axis=0)
        aa = _mm_nt(jnp.concatenate([al, rt], axis=0), rhs)
        yield
        n_ab = jnp.where(strict, aa[:c2, :c2], 0.0)
        a_ak = jnp.where(strict, aa[:c2, c2:], 0.0)
        a_rb = jnp.where(incl, aa[c2:, :c2], 0.0)
        a_rk = jnp.where(incl, aa[c2:, c2:], 0.0)
        vs = stack(v)
        x = jnp.concatenate([al, _mm(a_ak, vs)], axis=1)
        yield
        pw = n_ab
        x = x + _mm(pw, x)
        yield
        for _ in range(int(math.log2(c_rows)) - 1):
            pw = _mm(pw, pw)
            yield
            x = x + _mm(pw, x)
            yield
        w, u0 = x[:, :LANES], x[:, LANES:]
        y = _mm(a_rb, x)
        yield
        ro = rt + y[:, :LANES]
        o0 = y[:, LANES:] + _mm(a_rk, vs)
        yield
        p = _mm_tn(w, bg)
        yield
        q = _mm_tn(jnp.concatenate([u0, vs], axis=0), rhs)
        yield
        gam = jnp.exp(cl[c_rows - 1:c_rows, :])
        bonus = head_sum(r * k * rkw_ref[:, lanes]) * v
        return ro, o0, p, q, gam, bonus

    def finish(rows, lanes, o2, bonus):
        o = o2[:c_rows] + o2[c_rows:]
        mean = head_sum(o) * inv_n
        dlt = o - mean
        var = head_sum(dlt * dlt) * inv_n
        on =dlt * lax.rsqrt(var + RWKV_GN_EPS) * lng_ref[:, lanes] + lnb_ref[:, lanes]
        o_ref[rows, lanes] = (on + bonus) * gate_ref[rows, lanes]

    def group(g, carry):
        r0 = pl.multiple_of(g * (ngrp * c_rows), ngrp * c_rows)
        lanes = [slice(pi * LANES, (pi + 1) * LANES) for pi in range(npb)]
        rows = [pl.ds(r0 + c * c_rows, c_rows) for c in range(ngrp)]
        units = [(c, pi) for c in range(ngrp) for pi in range(npb)]
        prep = dict(zip(units, _lockstep([prepare(rows[c], lanes[pi]) for c, pi in units])))
        s = [s_ref[pi] for pi in range(npb)]
        for c, pi in units:
            ro, o0, p, q, gam, bonus = prep[c, pi]
            finish(rows[c], lanes[pi], _mm_nt(ro, s[pi]) + o0, bonus)
            s[pi] = (s[pi] + _mm(s[pi], p) + q) * gam
        for pi in range(npb):
            s_ref[pi] = s[pi]
        return carry

    lax.fori_loop(0, nchunk // ngrp, group, 0)

    @pl.when(i == pl.num_programs(2) - 1)
    def _():
        sout_ref[...] = s_ref[...]


def _rwkv_recurrence(rkv, lw, a, gate, kkw, kaw, rkw, lng, lnb, s0_blk, b, t):
    d = lw.shape[1]
    npair = d // LANES
    c_rows = min(CHUNK, t)
    tb = _tile(t, 512)
    nb = t // tb
    nchunk = tb // c_rows
    npb = 2 if npair % 2 == 0 else 1
    ngrp = _tile(nchunk, RWKV_CHUNKS_IN_FLIGHT)
    bw = npb * LANES
    kern = functools.partial(_rwkv_kernel, c_rows=c_rows, nchunk=nchunk, npb=npb, ngrp=ngrp)
    rowblk = pl.BlockSpec((tb, bw), lambda bi, p, i: (bi * nb + i, p))
    rkvblk = lambda c: pl.BlockSpec((None, tb, bw), lambda bi, p, i: (c, bi * nb + i, p))
    vec = pl.BlockSpec((1, bw), lambda bi, p, i: (0, p))
    st = pl.BlockSpec((None, npb, LANES, LANES), lambda bi, p, i: (bi, p, 0, 0))
    return pl.pallas_call(
        kern,
        grid=(b, npair // npb, nb),
        in_specs=[rkvblk(0), rkvblk(1), rkvblk(2), rowblk, rowblk, rowblk, vec, vec, vec, vec, vec, st],
        out_specs=[rowblk, st],
        out_shape=[jax.ShapeDtypeStruct((b * t, d), F32),
                   jax.ShapeDtypeStruct((b, npair, LANES, LANES), F32)],
        scratch_shapes=[pltpu.VMEM((npb, LANES, LANES), F32)],
        compiler_params=_params("parallel", "parallel", "arbitrary"),
        name="rwkv_recurrence",
    )(rkv, rkv, rkv, lw, a, gate, kkw.reshape(1, d), kaw.reshape(1, d), rkw.reshape(1, d),
      lng.reshape(1, d), lnb.reshape(1, d), s0_blk)


def _to_blockdiag(s):
    b, h, n, _ = s.shape
    s = s.reshape(b, h // 2, 2, n, n)
    z = jnp.zeros_like(s[:, :, 0])
    top = jnp.concatenate([s[:, :, 0], z], axis=-1)
    bot = jnp.concatenate([z, s[:, :, 1]], axis=-1)
    return jnp.concatenate([top, bot], axis=-2)


def _from_blockdiag(sb):
    b, p, n2, _ = sb.shape
    n = n2 // 2
    return jnp.stack([sb[:, :, :n, :n], sb[:, :, n:, n:]], axis=2).reshape(b, 2 * p, n, n)


def _trunk(x3, hgrn_s0, k_cache, v_cache, wkv_s0, shift0, w, lower):
    b, t, d = x3.shape
    x = x3.reshape(b * t, d)
    depth = w['norm_g'].shape[0]
    hgrn_out, k_out, v_out, wkv_out, shift_out = [], [], [], [], []
    xn = _norm(x, w['norm_g'][0, 0], MXU_DTYPE)
    for layer in range(depth):
        kind, j = layer % 3, layer // 3
        g = w['norm_g'][layer]
        last = layer == depth - 1
        x, n = _ffn(x, xn, *w['ffn'][layer][0], g[1], F32 if kind == 2 else MXU_DTYPE)
        if kind == 0:
            proj = _proj(n, w['hgrn_w_in'][j])
            s0_t = jnp.swapaxes(hgrn_s0[j], -1, -2)
            o, s_t = _hgrn_recurrence(proj, lower[layer], s0_t, b, t)
            hgrn_out.append(jnp.swapaxes(s_t, -1, -2))
            x, xn = _gated_proj_res(x, o, proj, 3, w['hgrn_norm_g'][j], w['hgrn_w_out'][j], g[2], MXU_DTYPE)
        elif kind == 1:
            qkv = _proj(n, w['attn_w_qkv'][j])
            rel_bias = w['attn_rel_bias'][j]
            nh = rel_bias.shape[0]
            if k_cache is None:
                past = PREV_CHUNKS * CHUNK
                o = _attn_prompt(qkv, _rel_bias(rel_bias, CHUNK, BAND, past), b, t)
                kept = min(past, t)
            else:
                rows = k_cache.shape[2]
                bias = _rel_bias(rel_bias, t, rows + t, rows)
                o = _attn_cached(qkv, k_cache[j].reshape(b, rows, d), v_cache[j].reshape(b, rows, d),
                                 bias[:, :, :rows], bias[:, :, rows:], b, t)
                kept = t
            qkv3 = qkv.reshape(b, t, 3 * d)
            keep = lambda c: qkv3[:, t - kept:, c * d:(c + 1) * d].reshape(b, kept, nh, ATTN_HEAD_DIM)
            k_out.append(keep(1))
            v_out.append(keep(2))
            x, xn = _proj_res(x, o, w['attn_w_out'][j], g[2], MXU_DTYPE)
        else:
            h = n
            h3 = h.reshape(b, t, d)
            h_prev = jnp.concatenate([shift0[j].astype(F32), h3[:, :-1]], axis=1).reshape(b * t, d)
            mu = w['rwkv_mu'][j]
            rkv = _rwkv_rkv(h, h_prev, mu[:3], w['rwkv_w_rkv'][j])
            lw, a, gate = _rwkv_lora(h, h_prev, mu[3:], w['rwkv_w0'][j], w['rwkv_w1'][j], w['rwkv_w2'][j],
                                     w['rwkv_a0'][j], w['rwkv_a1'][j], w['rwkv_a2'][j],
                                     w['rwkv_g1'][j], w['rwkv_g2'][j])
            o, s_blk = _rwkv_recurrence(rkv, lw, a, gate, w['rwkv_k_k'][j], w['rwkv_k_a'][j],
                                        w['rwkv_r_k'][j].reshape(-1), w['rwkv_ln_g'][j], w['rwkv_ln_b'][j],
                                        _to_blockdiag(wkv_s0[j]), b, t)
            shift_out.append(h3[:, -1:])
            wkv_out.append(_from_blockdiag(s_blk))
            x, xn = _proj_res(x, o, w['rwkv_w_out'][j], g[2], MXU_DTYPE)
        g_next = w['final_norm_g'] if last else w['norm_g'][layer + 1, 0]
        x, xn = _ffn(x, xn, *w['ffn'][layer][1], g_next, F32 if last else MXU_DTYPE, keep_x=not last)
    y = xn.reshape(b, t, d)
    return (y,jnp.stack(hgrn_out), jnp.stack(k_out), jnp.stack(v_out),
            jnp.stack(wkv_out), jnp.stack(shift_out))


def kernel(x_prompt, x_sample, state_hgrn, cache_k_band, cache_v_band, state_wkv, state_shift,
           norm_g, final_norm_g, ffn_w_gate_up, ffn_w_down,
           hgrn_w_in, hgrn_lb_logits, hgrn_norm_g, hgrn_w_out,
           attn_w_qkv, attn_rel_bias, attn_w_out,
           rwkv_mu, rwkv_w_rkv, rwkv_w0, rwkv_w1, rwkv_w2, rwkv_a0, rwkv_a1, rwkv_a2,
           rwkv_g1, rwkv_g2, rwkv_k_k, rwkv_k_a, rwkv_r_k, rwkv_ln_g, rwkv_ln_b, rwkv_w_out):
    depth = norm_g.shape[0]
    cast = lambda a: a.astype(MXU_DTYPE)
    ffn = [[_prep_ffn_weights(ffn_w_gate_up[l, i], ffn_w_down[l, i]) for i in range(2)] for l in range(depth)]
    w = dict(norm_g=norm_g, final_norm_g=final_norm_g, ffn=ffn,
             hgrn_w_in=cast(hgrn_w_in), hgrn_norm_g=hgrn_norm_g, hgrn_w_out=cast(hgrn_w_out),
             attn_w_qkv=cast(attn_w_qkv), attn_rel_bias=attn_rel_bias, attn_w_out=cast(attn_w_out),
             rwkv_mu=rwkv_mu, rwkv_w_rkv=cast(rwkv_w_rkv), rwkv_w0=rwkv_w0, rwkv_w1=cast(rwkv_w1),
             rwkv_w2=cast(rwkv_w2), rwkv_a0=rwkv_a0, rwkv_a1=cast(rwkv_a1), rwkv_a2=cast(rwkv_a2),
             rwkv_g1=cast(rwkv_g1), rwkv_g2=cast(rwkv_g2), rwkv_k_k=rwkv_k_k, rwkv_k_a=rwkv_k_a,
             rwkv_r_k=rwkv_r_k, rwkv_ln_g=rwkv_ln_g, rwkv_ln_b=rwkv_ln_b, rwkv_w_out=cast(rwkv_w_out))
    probs = jax.nn.softmax(hgrn_lb_logits.astype(F32), axis=0)
    lower = jnp.cumsum(probs, axis=0) - probs[0]

    b = x_prompt.shape[0]
    d = x_prompt.shape[2]
    n_a, n_c = state_hgrn.shape[0], state_wkv.shape[0]
    hgrn0 = jnp.zeros((n_a, b) + state_hgrn.shape[2:], F32)
    wkv0 = jnp.zeros((n_c, b) + state_wkv.shape[2:], F32)
    shift0 = jnp.zeros((n_c, b, 1, d), F32)
    y_p, hgrn_p, k_p, v_p, wkv_p, shift_p = _trunk(x_prompt, hgrn0, None, None, wkv0, shift0, w, lower)
    y_s, hgrn_s, k_s, v_s, wkv_s, shift_s = _trunk(x_sample, state_hgrn, cache_k_band, cache_v_band,
                                                   state_wkv, state_shift, w, lower)
    return (y_p, y_s, hgrn_p, hgrn_s, k_p, v_p, k_s, v_s, wkv_p, wkv_s, shift_p, shift_s)
```

```python
import functools
import math

import numpy as np
import jax
import jax.numpy as jnp
from jax import lax
from jax.experimental import pallas as pl
from jax.experimental.pallas import tpu as pltpu

F32 = jnp.float32
MXU_DTYPE = jnp.bfloat16

LANES = 128
CHUNK = 64
PREV_CHUNKS = 8
BAND = (PREV_CHUNKS + 1) * CHUNK
REL_CLIP = 128
NORM_EPS = 1e-6
RWKV_GN_EPS = 64e-5
HGRN_HEAD = 128
HGRN_BLOCKS_IN_FLIGHT = 4
HGRN_FINE_LEVELS = (8, 4)
ATTN_HEAD_DIM = 64
ATTN_CHUNKS_IN_FLIGHT = 4
RWKV_HEAD = 64
RWKV_CHUNKS_IN_FLIGHT = 8
STREAM_ROWS_PER_STEP = 128
VMEM_LIMIT = 60 * 1024 * 1024


def _params(*sem):
    return pltpu.CompilerParams(dimension_semantics=sem, vmem_limit_bytes=VMEM_LIMIT)


def _tile(n, pref):
    if n <= pref:
        return n
    t = pref
    while n % t:
        t //= 2
    return t


def _streams_per_block(b, t, tb):
    if tb < t:
        return 1
    ns = b
    while ns > 1 and ns * t > STREAM_ROWS_PER_STEP:
        ns //= 2
    return ns if b % ns == 0 else 1


def _mm(a, b):
    return jnp.dot(a.astype(MXU_DTYPE), b.astype(MXU_DTYPE), preferred_element_type=F32)


def _mm_nt(a, b):
    return lax.dot_general(a.astype(MXU_DTYPE), b.astype(MXU_DTYPE),
                           (((1,), (1,)), ((), ())), preferred_element_type=F32)


def _mm_tn(a, b):
    return lax.dot_general(a.astype(MXU_DTYPE), b.astype(MXU_DTYPE),
                           (((0,), (0,)), ((), ())), preferred_element_type=F32)


def _split3(x):
    hi = x.astype(jnp.bfloat16)
    r1 = x - hi.astype(F32)
    mid = r1.astype(jnp.bfloat16)
    lo = (r1 - mid.astype(F32)).astype(jnp.bfloat16)
    return hi, mid, lo


def _sel_mm(sel, x):
    sel = sel.astype(jnp.bfloat16)
    hi, mid, lo = _split3(x)
    d = lambda p: jnp.dot(sel, p, preferred_element_type=F32)
    return d(hi) + (d(mid) + d(lo))


def _cumsum_rows(x):
    row = lax.broadcasted_iota(jnp.int32, x.shape, 0)
    shift = 1
    while shift < x.shape[0]:
        x = x + jnp.where(row >= shift, pltpu.roll(x, shift, axis=0), 0.0)
        shift *= 2
    return x


def _lockstep(gens):
    out = [None] * len(gens)
    live = list(range(len(gens)))
    while live:
        for n in list(live):
            try:
                next(gens[n])
            except StopIteration as stop:
                out[n] = stop.value
                live.remove(n)
    return out


def _rms(x):
    return x * lax.rsqrt(jnp.mean(x * x, axis=-1, keepdims=True) + NORM_EPS)


def _sigmoid(x):
    return 1.0 / (1.0 + jnp.exp(-x))


def _silu(x):
    return x * _sigmoid(x)


def _ffn_up_kernel(xn_ref, wg_ref, wu_ref, h_ref, *, tail):
    j = pl.program_id(1)

    def columns(width):
        tm = xn_ref.shape[0]
        sub = _tile(tm, 512)
        for r in range(0, tm, sub):
            xn = xn_ref[r:r + sub, :]
            gate = jnp.dot(xn, wg_ref[:, :width], preferred_element_type=F32)
            up = jnp.dot(xn, wu_ref[:, :width], preferred_element_type=F32)
            h_ref[r:r + sub, :width] = (_silu(gate) * up).astype(h_ref.dtype)

    last = pl.num_programs(1) - 1
    pl.when(j < last)(lambda: columns(h_ref.shape[1]))
    pl.when(j == last)(lambda: columns(tail))


def _ffn(x, xn, w_gate, w_up, w_d, g_next, n_dtype, keep_x=True):
    t, d = x.shape
    f = w_d.shape[0]
    tf = min(f, 1024)
    nf = pl.cdiv(f, tf)
    tm = _tile(t, 2048)
    h = pl.pallas_call(
        functools.partial(_ffn_up_kernel, tail=f - (nf - 1) * tf),
        grid=(t // tm, nf),
        in_specs=[
            pl.BlockSpec((tm, d), lambda i, j: (i, 0)),
            pl.BlockSpec((d, tf), lambda i, j: (0, j)),
            pl.BlockSpec((d, tf), lambda i, j: (0, j)),
        ],
        out_specs=pl.BlockSpec((tm, tf), lambda i, j: (i, j)),
        out_shape=jax.ShapeDtypeStruct((t, f), MXU_DTYPE),
        compiler_params=_params("parallel", "parallel"),
        name="ffn_up",
    )(xn, w_gate, w_up)
    return _proj_res(x, h, w_d, g_next, n_dtype, scale=0.5, keep_x=keep_x)


def _prep_ffn_weights(w_gate_up, w_down):
    f = w_down.shape[0]
    return (w_gate_up[:, :f].astype(MXU_DTYPE), w_gate_up[:, f:].astype(MXU_DTYPE),
            w_down.astype(MXU_DTYPE))


def _proj_kernel(xn_ref, w_ref, o_ref):
    o_ref[...] = jnp.dot(xn_ref[...], w_ref[...], preferred_element_type=F32)


def _proj(xn, w):
    t, d = xn.shape
    n = w.shape[1]
    tm, tn = _tile(t, 1024), _tile(n, 2048)
    return pl.pallas_call(
        _proj_kernel,
        grid=(t // tm, n // tn),
        in_specs=[
            pl.BlockSpec((tm, d), lambda i, j: (i, 0)),
            pl.BlockSpec((d, tn), lambda i, j: (0, j)),
        ],
        out_specs=pl.BlockSpec((tm, tn), lambda i, j: (i, j)),
        out_shape=jax.ShapeDtypeStruct((t, n), F32),
        compiler_params=_params("parallel", "parallel"),
        name="proj",
    )(xn, w)


def _norm_kernel(x_ref, g_ref, o_ref):
    o_ref[...] = (_rms(x_ref[...]) * g_ref[...]).astype(o_ref.dtype)


def _norm(x, g, dtype):
    t, d = x.shape
    tm = _tile(t, 512)
    return pl.pallas_call(
        _norm_kernel,
        grid=(t // tm,),
        in_specs=[pl.BlockSpec((tm, d), lambda i: (i, 0)), pl.BlockSpec((1, d), lambda i: (0, 0))],
        out_specs=pl.BlockSpec((tm, d), lambda i: (i, 0)),
        out_shape=jax.ShapeDtypeStruct((t, d), dtype),
        compiler_params=_params("parallel"),
        name="norm",
    )(x, g.reshape(1, d))


def _emit(y, gn_ref, o_ref, n_ref, rows=slice(None)):
    if o_ref is not None:
        o_ref[rows, :] = y
    n_ref[rows, :] = (_rms(y) * gn_ref[...]).astype(n_ref.dtype)


def _proj_res_kernel(x_ref, a_ref, w_ref, gn_ref, *out_refs, scale):
    o_ref, n_ref = out_refs if len(out_refs) == 2 else (None, out_refs[0])
    y = _mm(a_ref[...], w_ref[...])
    _emit(x_ref[...] + (y if scale == 1.0 else scale * y), gn_ref, o_ref, n_ref)


def _gated_proj_res_kernel(x_ref, a_ref, z_ref, g_ref, w_ref, gn_ref, o_ref, n_ref, *, sub):
    for r in range(0, x_ref.shape[0], sub):
        rows = slice(r, r + sub)
        an = _rms(a_ref[rows, :]) * g_ref[...] * _silu(z_ref[rows, :])
        _emit(x_ref[rows, :] + _mm(an, w_ref[...]), gn_ref, o_ref, n_ref, rows)


def _resident(shape):
    return pl.BlockSpec(shape, lambda i: (0,) * len(shape), pipeline_mode=pl.Buffered(1))


def _proj_res(x, a, w, g_next, n_dtype, scale=1.0, keep_x=True):
    t, d = x.shape
    k = a.shape[1]
    tm = _tile(t, 512)
    row = pl.BlockSpec((tm, d), lambda i: (i, 0))
    outs = pl.pallas_call(
        functools.partial(_proj_res_kernel, scale=scale),
        grid=(t // tm,),
        in_specs=[row, pl.BlockSpec((tm, k), lambda i: (i, 0)), _resident((k, d)), _resident((1, d))],
        out_specs=[row] * (1 + keep_x),
        out_shape=[jax.ShapeDtypeStruct((t, d), F32)] * keep_x + [jax.ShapeDtypeStruct((t, d), n_dtype)],
        compiler_params=_params("parallel"),
        name="proj_res",
    )(x, a, w, g_next.reshape(1, d))
    return tuple(outs) if keep_x else (None, outs[0])


def _gated_proj_res(x, a, z_src, z_col, g, w, g_next, n_dtype):
    t, d = x.shape
    tm = _tile(t, 512)
    row = lambda col: pl.BlockSpec((tm, d), lambda i: (i, col))
    return pl.pallas_call(
        functools.partial(_gated_proj_res_kernel, sub=_tile(tm, 128)),
        grid=(t // tm,),
        in_specs=[row(0), row(0), row(z_col), _resident((1, d)), _resident((d, d)), _resident((1, d))],
        out_specs=[row(0), row(0)],
        out_shape=[jax.ShapeDtypeStruct((t, d), F32), jax.ShapeDtypeStruct((t, d), n_dtype)],
        compiler_params=_params("parallel"),
        name="gated_proj_res",
    )(x, a, z_src, g.reshape(1, d), w, g_next.reshape(1, d))


def _hgrn_levels(rows):
    t = np.arange(rows)[:, None]
    r = np.arange(rows)[None, :]
    mats = [r <= t]
    for b in HGRN_FINE_LEVELS:
        mid = (t // b) * b + b // 2 - 1
        upper = (t % b) >= b // 2
        mats.append(np.where(upper, (r > mid) & (r <= t), (r > t) & (r <= mid)))
    return np.concatenate(mats, axis=0).astype(np.float32)


def _hgrn_kernel(q_ref, f_ref, v_ref, lb_ref, lvl_ref, s0_ref, o_ref, sout_ref, st_ref, *,
                 rows, nblk, nstream):
    i = pl.program_id(2)

    @pl.when(i == 0)
    def _():
        for s in range(nstream):
            st_ref[s] = s0_ref[s, 0].T

    lb = lb_ref[...]
    lvl = lvl_ref[...]
    t_id = lax.broadcasted_iota(jnp.int32, (rows, HGRN_HEAD), 0)
    ti = lax.broadcasted_iota(jnp.int32, (rows, rows), 0)
    si = lax.broadcasted_iota(jnp.int32, (rows, rows), 1)
    scale = HGRN_HEAD ** -0.5

    def level(scores, b, qe, ke):
        upper = (t_id & (b - 1)) >= b // 2
        s_l = _mm_nt(jnp.where(upper, qe, 0.0), jnp.where(upper, 0.0, ke))
        if b < rows:
            sh = int(math.log2(b))
            s_l = jnp.where((ti >> sh) == (si >> sh), s_l, 0.0)
        return scores + s_l

    def block(sl):
        q = _silu(q_ref[sl, :]) * scale
        fz = f_ref[sl, :]
        v = v_ref[sl, :]
        f = lb + (1.0 - lb) * _sigmoid(fz)
        k = (1.0 - lb) * _sigmoid(-fz)
        sums = _sel_mm(lvl, jnp.log(f))
        yield
        g = sums[:rows]
        scores = jnp.where(ti == si, _mm_nt(q, k), 0.0)
        yield
        scores = level(scores, 2, q * f, k)
        yield
        for n, b in enumerate(HGRN_FINE_LEVELS):
            if b <= rows:
                e = jnp.exp(sums[(n + 1) * rows:(n + 2) * rows])
                scores = level(scores, b, q * e, k * e)
                yield
        b = 2 * HGRN_FINE_LEVELS[0]
        while b <= rows:
            g_mid = jnp.concatenate(
                [jnp.broadcast_to(g[m:m + 1, :], (b, HGRN_HEAD)) for m in range(b // 2 - 1, rows, b)], axis=0)
            upper = (t_id & (b - 1)) >= b // 2
            e = jnp.exp(jnp.where(upper, g - g_mid, g_mid - g))
            scores = level(scores, b, q * e, k * e)
            yield
            b *= 2
        g_last = g[rows - 1:rows, :]
        o_intra = _mm(scores, v)
        yield
        kv = _mm_tn(v, k * jnp.exp(g_last - g))
        return o_intra, q * jnp.exp(g), jnp.exp(g_last), kv

    units = [(s, slice((s * nblk + u) * rows, (s * nblk + u + 1) * rows))
             for s in range(nstream) for u in range(nblk)]
    st = [st_ref[s] for s in range(nstream)]
    for u0 in range(0, len(units), HGRN_BLOCKS_IN_FLIGHT):
        group = units[u0:u0 + HGRN_BLOCKS_IN_FLIGHT]
        blocks = _lockstep([block(sl) for _, sl in group])
        for (s, sl), (o_intra, qg, dec, kv) in zip(group, blocks):
            o_ref[sl, :] = o_intra + _mm_nt(qg, st[s])
            st[s] = st[s] * dec + kv
    for s in range(nstream):
        st_ref[s] = st[s]

    @pl.when(i == pl.num_programs(2) - 1)
    def _():
        for s in range(nstream):
            sout_ref[s, 0] = st[s].T


def _hgrn_recurrence(proj, lb, s0, b, t):
    d = proj.shape[1] // 4
    h = d // HGRN_HEAD
    tb = _tile(t, 1024)
    nb = t // tb
    rows = _tile(tb, 128)
    ns = _streams_per_block(b, t, tb)
    lvl = jnp.asarray(_hgrn_levels(rows), dtype=jnp.bfloat16)
    kern = functools.partial(_hgrn_kernel, rows=rows, nblk=tb // rows, nstream=ns)
    row = lambda bi, hi, i: bi * nb + i
    blk = lambda col: pl.BlockSpec((ns * tb, HGRN_HEAD), lambda bi, hi, i: (row(bi, hi, i), col * h + hi))
    state = pl.BlockSpec((ns, 1, HGRN_HEAD, HGRN_HEAD), lambda bi, hi, i: (bi, hi, 0, 0))
    return pl.pallas_call(
        kern,
        grid=(b // ns, h, nb),
        in_specs=[
            blk(0), blk(1), blk(2),
            pl.BlockSpec((1, HGRN_HEAD), lambda bi, hi, i: (0, hi)),
            pl.BlockSpec(lvl.shape, lambda bi, hi, i: (0, 0)),
            state,
        ],
        out_specs=[blk(0), state],
        out_shape=[jax.ShapeDtypeStruct((b * t, d), F32),
                   jax.ShapeDtypeStruct((b, h, HGRN_HEAD, HGRN_HEAD), F32)],
        scratch_shapes=[pltpu.VMEM((ns, HGRN_HEAD, HGRN_HEAD), F32)],
        compiler_params=_params("parallel", "parallel", "arbitrary"),
        name="hgrn_recurrence",
    )(proj, proj, proj, lb.reshape(1, d), lvl, s0)


def _softmax_pv(parts):
    m = functools.reduce(jnp.maximum, [jnp.max(s, axis=-1, keepdims=True) for s, _ in parts])
    es = [jnp.exp(s - m) for s, _ in parts]
    den = functools.reduce(jnp.add, [jnp.sum(e, axis=-1, keepdims=True) for e in es])
    yield
    pv = functools.reduce(jnp.add, [_mm(e, v) for e, (_, v) in zip(es, parts)])
    yield
    return pv / den


def _attn_prompt_kernel(q_ref, kp_ref, kc_ref, vp_ref, vc_ref, bias_ref, o_ref, k_scr, v_scr, *, tq):
    i = pl.program_id(2)
    k_scr[0:tq, :] = kp_ref[...].astype(k_scr.dtype)
    k_scr[tq:2 * tq, :] = kc_ref[...].astype(k_scr.dtype)
    v_scr[0:tq, :] = vp_ref[...].astype(v_scr.dtype)
    v_scr[tq:2 * tq, :] = vc_ref[...].astype(v_scr.dtype)
    lane = lax.broadcasted_iota(jnp.int32, (CHUNK, LANES), 1)
    head_a = lane < ATTN_HEAD_DIM
    col = lax.broadcasted_iota(jnp.int32, (CHUNK, BAND), 1)
    scale = ATTN_HEAD_DIM ** -0.5
    past = PREV_CHUNKS * CHUNK

    def unit(j, hd, masked):
        q = q_ref[j * CHUNK:(j + 1) * CHUNK, :] * scale
        w0 = tq - past + j * CHUNK
        sel = head_a if hd == 0 else jnp.logical_not(head_a)
        s = _mm_nt(jnp.where(sel, q, 0.0), k_scr[w0:w0 + BAND, :]) + bias_ref[hd]
        yield
        if masked:
            s = jnp.where(col + ((i - 1) * tq + w0) >= 0, s, -jnp.inf)
        return (yield from _softmax_pv([(s, v_scr[w0:w0 + BAND, :])]))

    def run(masked):
        for j0 in range(0, tq // CHUNK, ATTN_CHUNKS_IN_FLIGHT):
            js = range(j0, min(j0 + ATTN_CHUNKS_IN_FLIGHT, tq // CHUNK))
            outs = _lockstep([unit(j, hd, masked) for j in js for hd in range(2)])
            for n, j in enumerate(js):
                o_ref[j * CHUNK:(j + 1) * CHUNK, :] = jnp.where(head_a, outs[2 * n], outs[2 * n + 1])

    @pl.when(i == 0)
    def _():
        run(True)

    @pl.when(i > 0)
    def _():
        run(False)


def _attn_prompt(qkv, bias, b, t):
    d = qkv.shape[1] // 3
    npair = d // LANES
    tq = _tile(t, 512)
    nq = t // tq
    assert tq >= PREV_CHUNKS * CHUNK and tq % CHUNK == 0
    cur = lambda bi, i: bi * nq + i
    prev = lambda bi, i: bi * nq + jnp.maximum(i - 1, 0)
    return pl.pallas_call(
        functools.partial(_attn_prompt_kernel, tq=tq),
        grid=(b, npair, nq),
        in_specs=[
            pl.BlockSpec((tq, LANES), lambda bi, p, i: (cur(bi, i), p)),
            pl.BlockSpec((tq, LANES), lambda bi, p, i: (prev(bi, i), npair + p)),
            pl.BlockSpec((tq, LANES), lambda bi, p, i: (cur(bi, i), npair + p)),
            pl.BlockSpec((tq, LANES), lambda bi, p, i: (prev(bi, i), 2 * npair + p)),
            pl.BlockSpec((tq, LANES), lambda bi, p, i: (cur(bi, i), 2 * npair + p)),
            pl.BlockSpec((2, CHUNK, BAND), lambda bi, p, i: (p, 0, 0)),
        ],
        out_specs=pl.BlockSpec((tq, LANES), lambda bi, p, i: (cur(bi, i), p)),
        out_shape=jax.ShapeDtypeStruct((b * t, d), F32),
        scratch_shapes=[pltpu.VMEM((2 * tq, LANES), MXU_DTYPE), pltpu.VMEM((2 * tq, LANES), MXU_DTYPE)],
        compiler_params=_params("parallel", "parallel", "arbitrary"),
        name="attn_prompt",
    )(qkv, qkv, qkv, qkv, qkv, bias)


def _attn_cached_kernel(q_ref, kn_ref, vn_ref, kc_ref, vc_ref, bc_ref, bn_ref, o_ref, *, t, nstream):
    lane = lax.broadcasted_iota(jnp.int32, (t, LANES), 1)
    head_a = lane < ATTN_HEAD_DIM
    scale = ATTN_HEAD_DIM ** -0.5

    def unit(sn, hd):
        rows = slice(sn * t, (sn + 1) * t)
        sel = head_a if hd == 0 else jnp.logical_not(head_a)
        qm = jnp.where(sel, q_ref[rows, :] * scale, 0.0)
        s_c = _mm_nt(qm, kc_ref[sn]) + bc_ref[hd]
        s_n = _mm_nt(qm, kn_ref[rows, :]) + bn_ref[hd]
        yield
        return (yield from _softmax_pv([(s_c, vc_ref[sn]), (s_n, vn_ref[rows, :])]))

    for s0 in range(0, nstream, ATTN_CHUNKS_IN_FLIGHT):
        streams = range(s0, min(s0 + ATTN_CHUNKS_IN_FLIGHT, nstream))
        outs = _lockstep([unit(sn, hd) for sn in streams for hd in range(2)])
        for n, sn in enumerate(streams):
            o_ref[sn * t:(sn + 1) * t, :] = jnp.where(head_a, outs[2 * n], outs[2 * n + 1])


def _attn_cached(qkv, k_cache, v_cache, bias_c, bias_n, b, t):
    d = qkv.shape[1] // 3
    npair = d // LANES
    rows = k_cache.shape[1]
    ns = _streams_per_block(b, t, t)
    new = lambda col: pl.BlockSpec((ns * t, LANES), lambda bi, p: (bi, col * npair + p))
    cache = pl.BlockSpec((ns, rows, LANES), lambda bi, p: (bi, 0, p))
    return pl.pallas_call(
        functools.partial(_attn_cached_kernel, t=t, nstream=ns),
        grid=(b // ns, npair),
        in_specs=[
            new(0), new(1), new(2), cache, cache,
            pl.BlockSpec((2, t, rows), lambda bi, p: (p, 0, 0)),
            pl.BlockSpec((2, t, t), lambda bi, p: (p, 0, 0)),
        ],
        out_specs=new(0),
        out_shape=jax.ShapeDtypeStruct((b * t, d), F32),
        compiler_params=_params("parallel", "parallel"),
        name="attn_cached",
    )(qkv, qkv, qkv, k_cache, v_cache, bias_c, bias_n)


def _rel_bias(rel_bias, nq, nk, offset):
    rel = jnp.arange(nq + nk - 1) + (offset - nk + 1)
    diag = rel_bias[:, jnp.clip(rel, -REL_CLIP, REL_CLIP) + REL_CLIP].astype(F32)
    rev = diag[:, ::-1]
    return jnp.stack([rev[:, nq - 1 - q:nq - 1 - q + nk] for q in range(nq)], axis=1)


def _rwkv_rkv_kernel(h_ref, p_ref, mu_ref, w_ref, o_ref, l_ref):
    @pl.when(pl.program_id(2) == 0)
    def _():
        h = h_ref[...]
        l_ref[...] = (h + (p_ref[...] - h) * mu_ref[0]).astype(l_ref.dtype)

    o_ref[0] = jnp.dot(l_ref[...], w_ref[0], preferred_element_type=F32)


def _rwkv_rkv(h, h_prev, mu, w_rkv):
    t, d = h.shape
    tm, tn = _tile(t, 512), _tile(d, 2048)
    return pl.pallas_call(
        _rwkv_rkv_kernel,
        grid=(3, t // tm, d // tn),
        in_specs=[
            pl.BlockSpec((tm, d), lambda c, i, j: (i, 0)),
            pl.BlockSpec((tm, d), lambda c, i, j: (i, 0)),
            pl.BlockSpec((1, 1, d), lambda c, i, j: (c, 0, 0)),
            pl.BlockSpec((1, d, tn), lambda c, i, j: (c, 0, j)),
        ],
        out_specs=pl.BlockSpec((1, tm, tn), lambda c, i, j: (c, i, j)),
        out_shape=jax.ShapeDtypeStruct((3, t, d), F32),
        scratch_shapes=[pltpu.VMEM((tm, d), MXU_DTYPE)],
        compiler_params=_params("parallel", "parallel", "arbitrary"),
        name="rwkv_rkv",
    )(h, h_prev, mu.reshape(-1, 1, d), w_rkv)


def _softplus(y):
    return jnp.maximum(y, 0.0) + jnp.log1p(jnp.exp(-jnp.abs(y)))


def _rwkv_lora_kernel(h_ref, p_ref, mu_ref, w0_ref, w1_ref, w2_ref, a0_ref, a1_ref, a2_ref,
                      g1_ref, g2_ref, lw_ref, a_ref, gate_ref):
    h = h_ref[...]
    xx = p_ref[...] - h
    lerp = lambda c: h + xx * mu_ref[c]
    z = w0_ref[...] + _mm(jnp.tanh(_mm(lerp(0), w1_ref[...])), w2_ref[...])
    w_log = -_softplus(-z) - 0.5
    lw_ref[...] = -jnp.exp(w_log)
    a_ref[...] = _sigmoid(a0_ref[...] + _mm(_mm(lerp(1), a1_ref[...]), a2_ref[...]))
    gate_ref[...] = _mm(_sigmoid(_mm(lerp(2), g1_ref[...])), g2_ref[...])


def _rwkv_lora(h, h_prev, mu_wag, w0, w1, w2, a0, a1, a2, g1, g2):
    t, d = h.shape
    tm = _tile(t, 256)
    row = pl.BlockSpec((tm, d), lambda i: (i, 0))
    full = lambda a: pl.BlockSpec(a.shape, lambda i: (0,) * a.ndim)
    consts = [mu_wag.reshape(3, 1, d), w0.reshape(1, d), w1, w2, a0.reshape(1, d), a1, a2, g1, g2]
    return pl.pallas_call(
        _rwkv_lora_kernel,
        grid=(t // tm,),
        in_specs=[row, row] + [full(c) for c in consts],
        out_specs=[row, row, row],
        out_shape=[jax.ShapeDtypeStruct((t, d), F32)] * 3,
        compiler_params=_params("parallel"),
        name="rwkv_lora",
    )(h, h_prev, *consts)


def _rwkv_kernel(r_ref, k_ref, v_ref, lw_ref, a_ref, gate_ref, kkw_ref, kaw_ref, rkw_ref,
                 lng_ref, lnb_ref, s0_ref, o_ref, sout_ref, s_ref, *, c_rows, nchunk, npb, ngrp, nstream):
    i = pl.program_id(2)
    c2 = 2 * c_rows

    @pl.when(i == 0)
    def _():
        s_ref[...] = s0_ref[...]

    lane = lax.broadcasted_iota(jnp.int32, (c_rows, LANES), 1)
    head_a = lane < RWKV_HEAD
    si = lax.broadcasted_iota(jnp.int32, (c2, c2), 0)
    sj = lax.broadcasted_iota(jnp.int32, (c2, c2), 1)
    same_blk = (si // c_rows) == (sj // c_rows)
    strict = jnp.logical_and(same_blk, (sj % c_rows) < (si % c_rows))
    incl = jnp.logical_and(same_blk, (sj % c_rows) <= (si % c_rows))
    inv_n = 1.0 / RWKV_HEAD

    def stack(x):
        return jnp.concatenate([jnp.where(head_a, x, 0.0), jnp.where(head_a, 0.0, x)], axis=0)

    def head_sum(x):
        sa = jnp.sum(jnp.where(head_a, x, 0.0), axis=-1, keepdims=True)
        sb = jnp.sum(jnp.where(head_a, 0.0, x), axis=-1, keepdims=True)
        return jnp.where(head_a, sa, sb)

    def prepare(rows, lanes):
        r, kr, v = r_ref[rows, lanes], k_ref[rows, lanes], v_ref[rows, lanes]
        lw, a = lw_ref[rows, lanes], a_ref[rows, lanes]
        kkw, kaw = kkw_ref[:, lanes], kaw_ref[:, lanes]
        kk = kr * kkw
        ss = head_sum(kk * kk)
        cl = _cumsum_rows(lw)
        yield
        kk = kk * lax.rsqrt(jnp.maximum(ss, 1e-24))
        k = kr * (1.0 + (a - 1.0) * kaw)
        e_neg = jnp.exp(-cl)
        al = stack(-kk * jnp.exp(cl - lw))
        rt = stack(r * jnp.exp(cl))
        bg = stack(a * kk * e_neg)
        rhs = jnp.concatenate([bg, stack(k * e_neg)], axis=0)
        aa = _mm_nt(jnp.concatenate([al, rt], axis=0), rhs)
        yield
        n_ab = jnp.where(strict, aa[:c2, :c2], 0.0)
        a_ak = jnp.where(strict, aa[:c2, c2:], 0.0)
        a_rb = jnp.where(incl, aa[c2:, :c2], 0.0)
        a_rk = jnp.where(incl, aa[c2:, c2:], 0.0)
        vs = stack(v)
        x = jnp.concatenate([al, _mm(a_ak, vs)], axis=1)
        yield
        pw = n_ab
        x = x + _mm(pw, x)
        yield
        for _ in range(int(math.log2(c_rows)) - 1):
            pw = _mm(pw, pw)
            yield
            x = x + _mm(pw, x)
            yield
        w, u0 = x[:, :LANES], x[:, LANES:]
        y = _mm(a_rb, x)
        yield
        ro = rt + y[:, :LANES]
        o0 = y[:, LANES:] + _mm(a_rk, vs)
        yield
        p = _mm_tn(w, bg)
        yield
        q = _mm_tn(jnp.concatenate([u0, vs], axis=0), rhs)
        yield
        gam = jnp.exp(cl[c_rows - 1:c_rows, :])
        bonus = head_sum(r * k * rkw_ref[:, lanes]) * v
        return ro, o0, p, q, gam, bonus

    def finish(rows, lanes, o2, bonus):
        o = o2[:c_rows] + o2[c_rows:]
        mean = head_sum(o) * inv_n
        dlt = o - mean
        var = head_sum(dlt * dlt) * inv_n
        on =dlt * lax.rsqrt(var + RWKV_GN_EPS) * lng_ref[:, lanes] + lnb_ref[:, lanes]
        o_ref[rows, lanes] = (on + bonus) * gate_ref[rows, lanes]

    def group(g, carry):
        r0 = pl.multiple_of(g * (ngrp * c_rows), ngrp * c_rows)
        lanes = [slice(pi * LANES, (pi + 1) * LANES) for pi in range(npb)]
        rows = {(sn, c): pl.ds(sn * (nchunk * c_rows) + r0 + c * c_rows, c_rows)
                for sn in range(nstream) for c in range(ngrp)}
        units = [(sn, c, pi) for c in range(ngrp) for sn in range(nstream) for pi in range(npb)]
        prep = dict(zip(units, _lockstep([prepare(rows[sn, c], lanes[pi]) for sn, c, pi in units])))
        s = {(sn, pi): s_ref[sn, pi] for sn in range(nstream) for pi in range(npb)}
        for sn, c, pi in units:
            ro, o0, p, q, gam, bonus = prep[sn, c, pi]
            finish(rows[sn, c], lanes[pi], _mm_nt(ro, s[sn, pi]) + o0, bonus)
            s[sn, pi] = (s[sn, pi] + _mm(s[sn, pi], p) + q) * gam
        for (sn, pi), val in s.items():
            s_ref[sn, pi] = val
        return carry

    lax.fori_loop(0, nchunk // ngrp, group, 0)

    @pl.when(i == pl.num_programs(2) - 1)
    def _():
        sout_ref[...] = s_ref[...]


def _rwkv_recurrence(rkv, lw, a, gate, kkw, kaw, rkw, lng, lnb, s0_blk, b, t):
    d = lw.shape[1]
    npair = d // LANES
    c_rows = min(CHUNK, t)
    tb = _tile(t, 512)
    nb = t // tb
    nchunk = tb // c_rows
    ns = _streams_per_block(b, t, tb)
    npb = 2 if npair % 2 == 0 else 1
    ngrp = _tile(nchunk, RWKV_CHUNKS_IN_FLIGHT)
    bw = npb * LANES
    kern = functools.partial(_rwkv_kernel, c_rows=c_rows, nchunk=nchunk, npb=npb, ngrp=ngrp, nstream=ns)
    rowblk = pl.BlockSpec((ns * tb, bw), lambda bi, p, i: (bi * nb + i, p))
    rkvblk = lambda c: pl.BlockSpec((None, ns * tb, bw), lambda bi, p, i: (c, bi * nb + i, p))
    vec = pl.BlockSpec((1, bw), lambda bi, p, i: (0, p))
    st = pl.BlockSpec((ns, npb, LANES, LANES), lambda bi, p, i: (bi, p, 0, 0))
    return pl.pallas_call(
        kern,
        grid=(b // ns, npair // npb, nb),
        in_specs=[rkvblk(0), rkvblk(1), rkvblk(2), rowblk, rowblk, rowblk, vec, vec, vec, vec, vec, st],
        out_specs=[rowblk, st],
        out_shape=[jax.ShapeDtypeStruct((b * t, d), F32),
                   jax.ShapeDtypeStruct((b, npair, LANES, LANES), F32)],
        scratch_shapes=[pltpu.VMEM((ns, npb, LANES, LANES), F32)],
        compiler_params=_params("parallel", "parallel", "arbitrary"),
        name="rwkv_recurrence",
    )(rkv, rkv, rkv, lw, a, gate, kkw.reshape(1, d), kaw.reshape(1, d), rkw.reshape(1, d),
      lng.reshape(1, d), lnb.reshape(1, d), s0_blk)


def _to_blockdiag(s):
    b, h, n, _ = s.shape
    s = s.reshape(b, h // 2, 2, n, n)
    z = jnp.zeros_like(s[:, :, 0])
    top = jnp.concatenate([s[:, :, 0], z], axis=-1)
    bot = jnp.concatenate([z, s[:, :, 1]], axis=-1)
    return jnp.concatenate([top, bot], axis=-2)


def _from_blockdiag(sb):
    b, p, n2, _ = sb.shape
    n = n2 // 2
    return jnp.stack([sb[:, :, :n, :n], sb[:, :, n:, n:]], axis=2).reshape(b, 2 * p, n, n)


def _trunk(x3, hgrn_s0, k_cache, v_cache, wkv_s0, shift0, w, lower):
    b, t, d = x3.shape
    x = x3.reshape(b * t, d)
    depth = w['norm_g'].shape[0]
    hgrn_out, k_out, v_out, wkv_out, shift_out = [], [], [], [], []
    xn = _norm(x, w['norm_g'][0, 0], MXU_DTYPE)
    for layer in range(depth):
        kind, j = layer % 3, layer // 3
        g = w['norm_g'][layer]
        last = layer == depth - 1
        x, n = _ffn(x, xn, *w['ffn'][layer][0], g[1], F32 if kind == 2 else MXU_DTYPE)
        if kind == 0:
            proj = _proj(n, w['hgrn_w_in'][j])
            o, s_fin = _hgrn_recurrence(proj, lower[layer], hgrn_s0[j], b, t)
            hgrn_out.append(s_fin)
            x, xn = _gated_proj_res(x, o, proj, 3, w['hgrn_norm_g'][j], w['hgrn_w_out'][j], g[2], MXU_DTYPE)
        elif kind == 1:
            qkv = _proj(n, w['attn_w_qkv'][j])
            rel_bias = w['attn_rel_bias'][j]
            nh = rel_bias.shape[0]
            if k_cache is None:
                past = PREV_CHUNKS * CHUNK
                o = _attn_prompt(qkv, _rel_bias(rel_bias, CHUNK, BAND, past), b, t)
                kept = min(past, t)
            else:
                rows = k_cache.shape[2]
                bias = _rel_bias(rel_bias, t, rows + t, rows)
                o = _attn_cached(qkv, k_cache[j].reshape(b, rows, d), v_cache[j].reshape(b, rows, d),
                                 bias[:, :, :rows], bias[:, :, rows:], b, t)
                kept = t
            qkv3 = qkv.reshape(b, t, 3 * d)
            keep = lambda c: qkv3[:, t - kept:, c * d:(c + 1) * d].reshape(b, kept, nh, ATTN_HEAD_DIM)
            k_out.append(keep(1))
            v_out.append(keep(2))
            x, xn = _proj_res(x, o, w['attn_w_out'][j], g[2], MXU_DTYPE)
        else:
            h = n
            h3 = h.reshape(b, t, d)
            h_prev = jnp.concatenate([shift0[j].astype(F32), h3[:, :-1]], axis=1).reshape(b * t, d)
            mu = w['rwkv_mu'][j]
            rkv = _rwkv_rkv(h, h_prev, mu[:3], w['rwkv_w_rkv'][j])
            lw, a, gate = _rwkv_lora(h, h_prev, mu[3:], w['rwkv_w0'][j], w['rwkv_w1'][j], w['rwkv_w2'][j],
                                     w['rwkv_a0'][j], w['rwkv_a1'][j], w['rwkv_a2'][j],
                                     w['rwkv_g1'][j], w['rwkv_g2'][j])
            o, s_blk = _rwkv_recurrence(rkv, lw, a, gate, w['rwkv_k_k'][j], w['rwkv_k_a'][j],
                                        w['rwkv_r_k'][j].reshape(-1), w['rwkv_ln_g'][j], w['rwkv_ln_b'][j],
                                        _to_blockdiag(wkv_s0[j]), b, t)
            shift_out.append(h3[:, -1:])
            wkv_out.append(_from_blockdiag(s_blk))
            x, xn = _proj_res(x, o, w['rwkv_w_out'][j], g[2], MXU_DTYPE)
        g_next = w['final_norm_g'] if last else w['norm_g'][layer + 1, 0]
        x, xn = _ffn(x, xn, *w['ffn'][layer][1], g_next, F32 if last else MXU_DTYPE, keep_x=not last)
    y = xn.reshape(b, t, d)
    return (y,jnp.stack(hgrn_out), jnp.stack(k_out), jnp.stack(v_out),
            jnp.stack(wkv_out), jnp.stack(shift_out))


def kernel(x_prompt, x_sample, state_hgrn, cache_k_band, cache_v_band, state_wkv, state_shift,
           norm_g, final_norm_g, ffn_w_gate_up, ffn_w_down,
           hgrn_w_in, hgrn_lb_logits, hgrn_norm_g, hgrn_w_out,
           attn_w_qkv, attn_rel_bias, attn_w_out,
           rwkv_mu, rwkv_w_rkv, rwkv_w0, rwkv_w1, rwkv_w2, rwkv_a0, rwkv_a1, rwkv_a2,
           rwkv_g1, rwkv_g2, rwkv_k_k, rwkv_k_a, rwkv_r_k, rwkv_ln_g, rwkv_ln_b, rwkv_w_out):
    depth = norm_g.shape[0]
    cast = lambda a: a.astype(MXU_DTYPE)
    ffn = [[_prep_ffn_weights(ffn_w_gate_up[l, i], ffn_w_down[l, i]) for i in range(2)] for l in range(depth)]
    w = dict(norm_g=norm_g, final_norm_g=final_norm_g, ffn=ffn,
             hgrn_w_in=cast(hgrn_w_in), hgrn_norm_g=hgrn_norm_g, hgrn_w_out=cast(hgrn_w_out),
             attn_w_qkv=cast(attn_w_qkv), attn_rel_bias=attn_rel_bias, attn_w_out=cast(attn_w_out),
             rwkv_mu=rwkv_mu, rwkv_w_rkv=cast(rwkv_w_rkv), rwkv_w0=rwkv_w0, rwkv_w1=cast(rwkv_w1),
             rwkv_w2=cast(rwkv_w2), rwkv_a0=rwkv_a0, rwkv_a1=cast(rwkv_a1), rwkv_a2=cast(rwkv_a2),
             rwkv_g1=cast(rwkv_g1), rwkv_g2=cast(rwkv_g2), rwkv_k_k=rwkv_k_k, rwkv_k_a=rwkv_k_a,
             rwkv_r_k=rwkv_r_k, rwkv_ln_g=rwkv_ln_g, rwkv_ln_b=rwkv_ln_b, rwkv_w_out=cast(rwkv_w_out))
    probs = jax.nn.softmax(hgrn_lb_logits.astype(F32), axis=0)
    lower = jnp.cumsum(probs, axis=0) - probs[0]

    b = x_prompt.shape[0]
    d = x_prompt.shape[2]
    n_a, n_c = state_hgrn.shape[0], state_wkv.shape[0]
    hgrn0 = jnp.zeros((n_a, b) + state_hgrn.shape[2:], F32)
    wkv0 = jnp.zeros((n_c, b) + state_wkv.shape[2:], F32)
    shift0 = jnp.zeros((n_c, b, 1, d), F32)
    y_p, hgrn_p, k_p, v_p, wkv_p, shift_p = _trunk(x_prompt, hgrn0, None, None, wkv0, shift0, w, lower)
    y_s, hgrn_s, k_s, v_s, wkv_s, shift_s = _trunk(x_sample, state_hgrn, cache_k_band, cache_v_band,
                                                   state_wkv, state_shift, w, lower)
    return (y_p, y_s, hgrn_p, hgrn_s, k_p, v_p, k_s, v_s, wkv_p, wkv_s, shift_p, shift_s)
```

```python
import functools
import math

import numpy as np
import jax
import jax.numpy as jnp
from jax import lax
from jax.experimental import pallas as pl
from jax.experimental.pallas import tpu as pltpu

F32 = jnp.float32
MXU_DTYPE = jnp.bfloat16

LANES = 128
CHUNK = 64
PREV_CHUNKS = 8
BAND = (PREV_CHUNKS + 1) * CHUNK
REL_CLIP = 128
NORM_EPS = 1e-6
RWKV_GN_EPS = 64e-5
HGRN_HEAD = 128
HGRN_BLOCKS_IN_FLIGHT = 4
HGRN_FINE_LEVELS = (8, 4)
ATTN_HEAD_DIM = 64
ATTN_CHUNKS_IN_FLIGHT = 4
RWKV_HEAD = 64
RWKV_CHUNKS_IN_FLIGHT = 8
CAST_MAX_ROWS = 256
STREAM_ROWS_PER_STEP = 128
VMEM_LIMIT = 60 * 1024 * 1024


def _params(*sem):
    return pltpu.CompilerParams(dimension_semantics=sem, vmem_limit_bytes=VMEM_LIMIT)


def _tile(n, pref):
    if n <= pref:
        return n
    t = pref
    while n % t:
        t //= 2
    return t


def _streams_per_block(b, t, tb):
    if tb < t:
        return 1
    ns = b
    while ns > 1 and ns * t > STREAM_ROWS_PER_STEP:
        ns //= 2
    return ns if b % ns == 0 else 1


def _mm(a, b):
    return jnp.dot(a.astype(MXU_DTYPE), b.astype(MXU_DTYPE), preferred_element_type=F32)


def _mm_nt(a, b):
    return lax.dot_general(a.astype(MXU_DTYPE), b.astype(MXU_DTYPE),
                           (((1,), (1,)), ((), ())), preferred_element_type=F32)


def _mm_tn(a, b):
    return lax.dot_general(a.astype(MXU_DTYPE), b.astype(MXU_DTYPE),
                           (((0,), (0,)), ((), ())), preferred_element_type=F32)


def _split3(x):
    hi = x.astype(jnp.bfloat16)
    r1 = x - hi.astype(F32)
    mid = r1.astype(jnp.bfloat16)
    lo = (r1 - mid.astype(F32)).astype(jnp.bfloat16)
    return hi, mid, lo


def _sel_mm(sel, x):
    sel = sel.astype(jnp.bfloat16)
    hi, mid, lo = _split3(x)
    d = lambda p: jnp.dot(sel, p, preferred_element_type=F32)
    return d(hi) + (d(mid) + d(lo))


def _cumsum_rows(x):
    row = lax.broadcasted_iota(jnp.int32, x.shape, 0)
    shift = 1
    while shift < x.shape[0]:
        x = x + jnp.where(row >= shift, pltpu.roll(x, shift, axis=0), 0.0)
        shift *= 2
    return x


def _lockstep(gens):
    out = [None] * len(gens)
    live = list(range(len(gens)))
    while live:
        for n in list(live):
            try:
                next(gens[n])
            except StopIteration as stop:
                out[n] = stop.value
                live.remove(n)
    return out


def _rms(x):
    return x * lax.rsqrt(jnp.mean(x * x, axis=-1, keepdims=True) + NORM_EPS)


def _sigmoid(x):
    return 1.0 / (1.0 + jnp.exp(-x))


def _silu(x):
    return x * _sigmoid(x)


def _ffn_up_kernel(xn_ref, wg_ref, wu_ref, *refs, tail, cast_down, cast_next):
    refs = list(refs)
    d32_ref = refs.pop(0) if cast_down else None
    gu32_ref = refs.pop(0) if cast_next else None
    h_ref = refs.pop(0)
    if cast_down:
        refs.pop(0)[...] = d32_ref[...].astype(MXU_DTYPE)
    if cast_next:
        f_next = gu32_ref.shape[1] // 2
        refs.pop(0)[...] = gu32_ref[:, :f_next].astype(MXU_DTYPE)
        refs.pop(0)[...] = gu32_ref[:, f_next:].astype(MXU_DTYPE)
    j = pl.program_id(1)

    def columns(width):
        tm = xn_ref.shape[0]
        sub = _tile(tm, 512)
        for r in range(0, tm, sub):
            xn = xn_ref[r:r + sub, :]
            gate = jnp.dot(xn, wg_ref[:, :width], preferred_element_type=F32)
            up = jnp.dot(xn, wu_ref[:, :width], preferred_element_type=F32)
            h_ref[r:r + sub, :width] = (_silu(gate) * up).astype(h_ref.dtype)

    last = pl.num_programs(1) - 1
    pl.when(j < last)(lambda: columns(h_ref.shape[1]))
    pl.when(j == last)(lambda: columns(tail))


def _cast_rows(nrows, steps):
    rb = -(-nrows // (16 * steps)) * 16
    return rb if rb <= CAST_MAX_ROWS else None


def _ffn(x, xn, wk, w_next, g_next, n_dtype, keep_x=True):
    t, d = x.shape
    f = wk['d32'].shape[0]
    tf = min(f, 1024)
    nf = pl.cdiv(f, tf)
    tm = _tile(t, 2048)
    steps = (t // tm) * nf
    rb_d = _cast_rows(f, steps) if wk['down'] is None else None
    rb_n = _cast_rows(d, steps) if w_next is not None and w_next['gate'] is None else None
    if wk['gate'] is None:
        wk['gate'] = wk['gu32'][:, :f].astype(MXU_DTYPE)
        wk['up'] = wk['gu32'][:, f:].astype(MXU_DTYPE)
    rows = lambda rb, n: (lambda i, j: (jnp.minimum(i * nf + j, pl.cdiv(n, rb) - 1), 0))
    ins = [xn, wk['gate'], wk['up']]
    in_specs = [pl.BlockSpec((tm, d), lambda i, j: (i, 0)),
                pl.BlockSpec((d, tf), lambda i, j: (0, j)),
                pl.BlockSpec((d, tf), lambda i, j: (0, j))]
    out_specs = [pl.BlockSpec((tm, tf), lambda i, j: (i, j))]
    out_shape = [jax.ShapeDtypeStruct((t, f), MXU_DTYPE)]
    if rb_d:
        ins.append(wk['d32'])
        in_specs.append(pl.BlockSpec((rb_d, d), rows(rb_d, f)))
        out_specs.append(pl.BlockSpec((rb_d, d), rows(rb_d, f)))
        out_shape.append(jax.ShapeDtypeStruct((f, d), MXU_DTYPE))
    if rb_n:
        f2 = w_next['gu32'].shape[1]
        ins.append(w_next['gu32'])
        in_specs.append(pl.BlockSpec((rb_n, f2), rows(rb_n, d)))
        out_specs += [pl.BlockSpec((rb_n, f2 // 2), rows(rb_n, d))] * 2
        out_shape += [jax.ShapeDtypeStruct((d, f2 // 2), MXU_DTYPE)] * 2
    outs = pl.pallas_call(
        functools.partial(_ffn_up_kernel, tail=f - (nf - 1) * tf, cast_down=bool(rb_d), cast_next=bool(rb_n)),
        grid=(t // tm, nf),
        in_specs=in_specs,
        out_specs=out_specs,
        out_shape=out_shape,
        compiler_params=_params("arbitrary", "arbitrary"),
        name="ffn_up",
    )(*ins)
    h = outs[0]
    if rb_d:
        wk['down'] = outs[1]
    elif wk['down'] is None:
        wk['down'] = wk['d32'].astype(MXU_DTYPE)
    if rb_n:
        w_next['gate'], w_next['up'] = outs[-2:]
    return _proj_res(x, h, wk['down'], g_next, n_dtype, scale=0.5, keep_x=keep_x)


def _proj_kernel(xn_ref, w_ref, o_ref):
    o_ref[...] = jnp.dot(xn_ref[...], w_ref[...], preferred_element_type=F32)


def _proj(xn, w):
    t, d = xn.shape
    n = w.shape[1]
    tm, tn = _tile(t, 1024), _tile(n, 2048)
    return pl.pallas_call(
        _proj_kernel,
        grid=(t // tm, n // tn),
        in_specs=[
            pl.BlockSpec((tm, d), lambda i, j: (i, 0)),
            pl.BlockSpec((d, tn), lambda i, j: (0, j)),
        ],
        out_specs=pl.BlockSpec((tm, tn), lambda i, j: (i, j)),
        out_shape=jax.ShapeDtypeStruct((t, n), F32),
        compiler_params=_params("parallel", "parallel"),
        name="proj",
    )(xn, w)


def _norm_kernel(x_ref, g_ref, o_ref):
    o_ref[...] = (_rms(x_ref[...]) * g_ref[...]).astype(o_ref.dtype)


def _norm(x, g, dtype):
    t, d = x.shape
    tm = _tile(t, 512)
    return pl.pallas_call(
        _norm_kernel,
        grid=(t // tm,),
        in_specs=[pl.BlockSpec((tm, d), lambda i: (i, 0)), pl.BlockSpec((1, d), lambda i: (0, 0))],
        out_specs=pl.BlockSpec((tm, d), lambda i: (i, 0)),
        out_shape=jax.ShapeDtypeStruct((t, d), dtype),
        compiler_params=_params("parallel"),
        name="norm",
    )(x, g.reshape(1, d))


def _emit(y, gn_ref, o_ref, n_ref, rows=slice(None)):
    if o_ref is not None:
        o_ref[rows, :] = y
    n_ref[rows, :] = (_rms(y) * gn_ref[...]).astype(n_ref.dtype)


def _proj_res_kernel(x_ref, a_ref, w_ref, gn_ref, *out_refs, scale):
    o_ref, n_ref = out_refs if len(out_refs) == 2 else (None, out_refs[0])
    y = _mm(a_ref[...], w_ref[...])
    _emit(x_ref[...] + (y if scale == 1.0 else scale * y), gn_ref, o_ref, n_ref)


def _gated_proj_res_kernel(x_ref, a_ref, z_ref, g_ref, w_ref, gn_ref, o_ref, n_ref, *, sub):
    for r in range(0, x_ref.shape[0], sub):
        rows = slice(r, r + sub)
        an = _rms(a_ref[rows, :]) * g_ref[...] * _silu(z_ref[rows, :])
        _emit(x_ref[rows, :] + _mm(an, w_ref[...]), gn_ref, o_ref, n_ref, rows)


def _resident(shape):
    return pl.BlockSpec(shape, lambda i: (0,) * len(shape), pipeline_mode=pl.Buffered(1))


def _proj_res(x, a, w, g_next, n_dtype, scale=1.0, keep_x=True):
    t, d = x.shape
    k = a.shape[1]
    tm = _tile(t, 512)
    row = pl.BlockSpec((tm, d), lambda i: (i, 0))
    outs = pl.pallas_call(
        functools.partial(_proj_res_kernel, scale=scale),
        grid=(t // tm,),
        in_specs=[row, pl.BlockSpec((tm, k), lambda i: (i, 0)), _resident((k, d)), _resident((1, d))],
        out_specs=[row] * (1 + keep_x),
        out_shape=[jax.ShapeDtypeStruct((t, d), F32)] * keep_x + [jax.ShapeDtypeStruct((t, d), n_dtype)],
        compiler_params=_params("parallel"),
        name="proj_res",
    )(x, a, w, g_next.reshape(1, d))
    return tuple(outs) if keep_x else (None, outs[0])


def _gated_proj_res(x, a, z_src, z_col, g, w, g_next, n_dtype):
    t, d = x.shape
    tm = _tile(t, 512)
    row = lambda col: pl.BlockSpec((tm, d), lambda i: (i, col))
    return pl.pallas_call(
        functools.partial(_gated_proj_res_kernel, sub=_tile(tm, 128)),
        grid=(t // tm,),
        in_specs=[row(0), row(0), row(z_col), _resident((1, d)), _resident((d, d)), _resident((1, d))],
        out_specs=[row(0), row(0)],
        out_shape=[jax.ShapeDtypeStruct((t, d), F32), jax.ShapeDtypeStruct((t, d), n_dtype)],
        compiler_params=_params("parallel"),
        name="gated_proj_res",
    )(x, a, z_src, g.reshape(1, d), w, g_next.reshape(1, d))


def _hgrn_levels(rows):
    t = np.arange(rows)[:, None]
    r = np.arange(rows)[None, :]
    mats = [r <= t]
    for b in HGRN_FINE_LEVELS:
        mid = (t // b) * b + b // 2 - 1
        upper = (t % b) >= b // 2
        mats.append(np.where(upper, (r > mid) & (r <= t), (r > t) & (r <= mid)))
    return np.concatenate(mats, axis=0).astype(np.float32)


def _hgrn_kernel(q_ref, f_ref, v_ref, lb_ref, lvl_ref, s0_ref, o_ref, sout_ref, st_ref, *,
                 rows, nblk, nstream):
    i = pl.program_id(2)

    @pl.when(i == 0)
    def _():
        for s in range(nstream):
            st_ref[s] = s0_ref[s, 0].T

    lb = lb_ref[...]
    lvl = lvl_ref[...]
    t_id = lax.broadcasted_iota(jnp.int32, (rows, HGRN_HEAD), 0)
    ti = lax.broadcasted_iota(jnp.int32, (rows, rows), 0)
    si = lax.broadcasted_iota(jnp.int32, (rows, rows), 1)
    scale = HGRN_HEAD ** -0.5

    def level(scores, b, qe, ke):
        upper = (t_id & (b - 1)) >= b // 2
        s_l = _mm_nt(jnp.where(upper, qe, 0.0), jnp.where(upper, 0.0, ke))
        if b < rows:
            sh = int(math.log2(b))
            s_l = jnp.where((ti >> sh) == (si >> sh), s_l, 0.0)
        return scores + s_l

    def block(sl):
        q = _silu(q_ref[sl, :]) * scale
        fz = f_ref[sl, :]
        v = v_ref[sl, :]
        f = lb + (1.0 - lb) * _sigmoid(fz)
        k = (1.0 - lb) * _sigmoid(-fz)
        sums = _sel_mm(lvl, jnp.log(f))
        yield
        g = sums[:rows]
        scores = jnp.where(ti == si, _mm_nt(q, k), 0.0)
        yield
        scores = level(scores, 2, q * f, k)
        yield
        for n, b in enumerate(HGRN_FINE_LEVELS):
            if b <= rows:
                e = jnp.exp(sums[(n + 1) * rows:(n + 2) * rows])
                scores = level(scores, b, q * e, k * e)
                yield
        b = 2 * HGRN_FINE_LEVELS[0]
        while b <= rows:
            g_mid = jnp.concatenate(
                [jnp.broadcast_to(g[m:m + 1, :], (b, HGRN_HEAD)) for m in range(b // 2 - 1, rows, b)], axis=0)
            upper = (t_id & (b - 1)) >= b // 2
            e = jnp.exp(jnp.where(upper, g - g_mid, g_mid - g))
            scores = level(scores, b, q * e, k * e)
            yield
            b *= 2
        g_last = g[rows - 1:rows, :]
        o_intra = _mm(scores, v)
        yield
        kv = _mm_tn(v, k * jnp.exp(g_last - g))
        return o_intra, q * jnp.exp(g), jnp.exp(g_last), kv

    units = [(s, slice((s * nblk + u) * rows, (s * nblk + u + 1) * rows))
             for s in range(nstream) for u in range(nblk)]
    st = [st_ref[s] for s in range(nstream)]
    for u0 in range(0, len(units), HGRN_BLOCKS_IN_FLIGHT):
        group = units[u0:u0 + HGRN_BLOCKS_IN_FLIGHT]
        blocks = _lockstep([block(sl) for _, sl in group])
        for (s, sl), (o_intra, qg, dec, kv) in zip(group, blocks):
            o_ref[sl, :] = o_intra + _mm_nt(qg, st[s])
            st[s] = st[s] * dec + kv
    for s in range(nstream):
        st_ref[s] = st[s]

    @pl.when(i == pl.num_programs(2) - 1)
    def _():
        for s in range(nstream):
            sout_ref[s, 0] = st[s].T


def _hgrn_recurrence(proj, lb, s0, b, t):
    d = proj.shape[1] // 4
    h = d // HGRN_HEAD
    tb = _tile(t, 1024)
    nb = t // tb
    rows = _tile(tb, 128)
    ns = _streams_per_block(b, t, tb)
    lvl = jnp.asarray(_hgrn_levels(rows), dtype=jnp.bfloat16)
    kern = functools.partial(_hgrn_kernel, rows=rows, nblk=tb // rows, nstream=ns)
    row = lambda bi, hi, i: bi * nb + i
    blk = lambda col: pl.BlockSpec((ns * tb, HGRN_HEAD), lambda bi, hi, i: (row(bi, hi, i), col * h + hi))
    state = pl.BlockSpec((ns, 1, HGRN_HEAD, HGRN_HEAD), lambda bi, hi, i: (bi, hi, 0, 0))
    return pl.pallas_call(
        kern,
        grid=(b // ns, h, nb),
        in_specs=[
            blk(0), blk(1), blk(2),
            pl.BlockSpec((1, HGRN_HEAD), lambda bi, hi, i: (0, hi)),
            pl.BlockSpec(lvl.shape, lambda bi, hi, i: (0, 0)),
            state,
        ],
        out_specs=[blk(0), state],
        out_shape=[jax.ShapeDtypeStruct((b * t, d), F32),
                   jax.ShapeDtypeStruct((b, h, HGRN_HEAD, HGRN_HEAD), F32)],
        scratch_shapes=[pltpu.VMEM((ns, HGRN_HEAD, HGRN_HEAD), F32)],
        compiler_params=_params("parallel", "parallel", "arbitrary"),
        name="hgrn_recurrence",
    )(proj, proj, proj, lb.reshape(1, d), lvl, s0)


def _softmax_pv(parts):
    m = functools.reduce(jnp.maximum, [jnp.max(s, axis=-1, keepdims=True) for s, _ in parts])
    es = [jnp.exp(s - m) for s, _ in parts]
    den = functools.reduce(jnp.add, [jnp.sum(e, axis=-1, keepdims=True) for e in es])
    yield
    pv = functools.reduce(jnp.add, [_mm(e, v) for e, (_, v) in zip(es, parts)])
    yield
    return pv / den


def _attn_prompt_kernel(q_ref, kp_ref, kc_ref, vp_ref, vc_ref, bias_ref, o_ref, k_scr, v_scr, *, tq):
    i = pl.program_id(2)
    k_scr[0:tq, :] = kp_ref[...].astype(k_scr.dtype)
    k_scr[tq:2 * tq, :] = kc_ref[...].astype(k_scr.dtype)
    v_scr[0:tq, :] = vp_ref[...].astype(v_scr.dtype)
    v_scr[tq:2 * tq, :] = vc_ref[...].astype(v_scr.dtype)
    lane = lax.broadcasted_iota(jnp.int32, (CHUNK, LANES), 1)
    head_a = lane < ATTN_HEAD_DIM
    col = lax.broadcasted_iota(jnp.int32, (CHUNK, BAND), 1)
    scale = ATTN_HEAD_DIM ** -0.5
    past = PREV_CHUNKS * CHUNK

    def unit(j, hd, masked):
        q = q_ref[j * CHUNK:(j + 1) * CHUNK, :] * scale
        w0 = tq - past + j * CHUNK
        sel = head_a if hd == 0 else jnp.logical_not(head_a)
        s = _mm_nt(jnp.where(sel, q, 0.0), k_scr[w0:w0 + BAND, :]) + bias_ref[hd]
        yield
        if masked:
            s = jnp.where(col + ((i - 1) * tq + w0) >= 0, s, -jnp.inf)
        return (yield from _softmax_pv([(s, v_scr[w0:w0 + BAND, :])]))

    def run(masked):
        for j0 in range(0, tq // CHUNK, ATTN_CHUNKS_IN_FLIGHT):
            js = range(j0, min(j0 + ATTN_CHUNKS_IN_FLIGHT, tq // CHUNK))
            outs = _lockstep([unit(j, hd, masked) for j in js for hd in range(2)])
            for n, j in enumerate(js):
                o_ref[j * CHUNK:(j + 1) * CHUNK, :] = jnp.where(head_a, outs[2 * n], outs[2 * n + 1])

    @pl.when(i == 0)
    def _():
        run(True)

    @pl.when(i > 0)
    def _():
        run(False)


def _attn_prompt(qkv, bias, b, t):
    d = qkv.shape[1] // 3
    npair = d // LANES
    tq = _tile(t, 512)
    nq = t // tq
    assert tq >= PREV_CHUNKS * CHUNK and tq % CHUNK == 0
    cur = lambda bi, i: bi * nq + i
    prev = lambda bi, i: bi * nq + jnp.maximum(i - 1, 0)
    return pl.pallas_call(
        functools.partial(_attn_prompt_kernel, tq=tq),
        grid=(b, npair, nq),
        in_specs=[
            pl.BlockSpec((tq, LANES), lambda bi, p, i: (cur(bi, i), p)),
            pl.BlockSpec((tq, LANES), lambda bi, p, i: (prev(bi, i), npair + p)),
            pl.BlockSpec((tq, LANES), lambda bi, p, i: (cur(bi, i), npair + p)),
            pl.BlockSpec((tq, LANES), lambda bi, p, i: (prev(bi, i), 2 * npair + p)),
            pl.BlockSpec((tq, LANES), lambda bi, p, i: (cur(bi, i), 2 * npair + p)),
            pl.BlockSpec((2, CHUNK, BAND), lambda bi, p, i: (p, 0, 0)),
        ],
        out_specs=pl.BlockSpec((tq, LANES), lambda bi, p, i: (cur(bi, i), p)),
        out_shape=jax.ShapeDtypeStruct((b * t, d), F32),
        scratch_shapes=[pltpu.VMEM((2 * tq, LANES), MXU_DTYPE), pltpu.VMEM((2 * tq, LANES), MXU_DTYPE)],
        compiler_params=_params("parallel", "parallel", "arbitrary"),
        name="attn_prompt",
    )(qkv, qkv, qkv, qkv, qkv, bias)


def _attn_cached_kernel(q_ref, kn_ref, vn_ref, kc_ref, vc_ref, bc_ref, bn_ref, o_ref, *, t, nstream):
    lane = lax.broadcasted_iota(jnp.int32, (t, LANES), 1)
    head_a = lane < ATTN_HEAD_DIM
    scale = ATTN_HEAD_DIM ** -0.5

    def unit(sn, hd):
        rows = slice(sn * t, (sn + 1) * t)
        sel = head_a if hd == 0 else jnp.logical_not(head_a)
        qm = jnp.where(sel, q_ref[rows, :] * scale, 0.0)
        s_c = _mm_nt(qm, kc_ref[sn]) + bc_ref[hd]
        s_n = _mm_nt(qm, kn_ref[rows, :]) + bn_ref[hd]
        yield
        return (yield from _softmax_pv([(s_c, vc_ref[sn]), (s_n, vn_ref[rows, :])]))

    for s0 in range(0, nstream, ATTN_CHUNKS_IN_FLIGHT):
        streams = range(s0, min(s0 + ATTN_CHUNKS_IN_FLIGHT, nstream))
        outs = _lockstep([unit(sn, hd) for sn in streams for hd in range(2)])
        for n, sn in enumerate(streams):
            o_ref[sn * t:(sn + 1) * t, :] = jnp.where(head_a, outs[2 * n], outs[2 * n + 1])


def _attn_cached(qkv, k_cache, v_cache, bias_c, bias_n, b, t):
    d = qkv.shape[1] // 3
    npair = d // LANES
    rows = k_cache.shape[1]
    ns = _streams_per_block(b, t, t)
    new = lambda col: pl.BlockSpec((ns * t, LANES), lambda bi, p: (bi, col * npair + p))
    cache = pl.BlockSpec((ns, rows, LANES), lambda bi, p: (bi, 0, p))
    return pl.pallas_call(
        functools.partial(_attn_cached_kernel, t=t, nstream=ns),
        grid=(b // ns, npair),
        in_specs=[
            new(0), new(1), new(2), cache, cache,
            pl.BlockSpec((2, t, rows), lambda bi, p: (p, 0, 0)),
            pl.BlockSpec((2, t, t), lambda bi, p: (p, 0, 0)),
        ],
        out_specs=new(0),
        out_shape=jax.ShapeDtypeStruct((b * t, d), F32),
        compiler_params=_params("parallel", "parallel"),
        name="attn_cached",
    )(qkv, qkv, qkv, k_cache, v_cache, bias_c, bias_n)


def _rel_bias(rel_bias, nq, nk, offset):
    rel = jnp.arange(nq + nk - 1) + (offset - nk + 1)
    diag = rel_bias[:, jnp.clip(rel, -REL_CLIP, REL_CLIP) + REL_CLIP].astype(F32)
    rev = diag[:, ::-1]
    return jnp.stack([rev[:, nq - 1 - q:nq - 1 - q + nk] for q in range(nq)], axis=1)


def _rwkv_rkv_kernel(h_ref, p_ref, mu_ref, w_ref, o_ref, l_ref):
    @pl.when(pl.program_id(2) == 0)
    def _():
        h = h_ref[...]
        l_ref[...] = (h + (p_ref[...] - h) * mu_ref[0]).astype(l_ref.dtype)

    o_ref[0] = jnp.dot(l_ref[...], w_ref[0], preferred_element_type=F32)


def _rwkv_rkv(h, h_prev, mu, w_rkv):
    t, d = h.shape
    tm, tn = _tile(t, 512), _tile(d, 2048)
    return pl.pallas_call(
        _rwkv_rkv_kernel,
        grid=(3, t // tm, d // tn),
        in_specs=[
            pl.BlockSpec((tm, d), lambda c, i, j: (i, 0)),
            pl.BlockSpec((tm, d), lambda c, i, j: (i, 0)),
            pl.BlockSpec((1, 1, d), lambda c, i, j: (c, 0, 0)),
            pl.BlockSpec((1, d, tn), lambda c, i, j: (c, 0, j)),
        ],
        out_specs=pl.BlockSpec((1, tm, tn), lambda c, i, j: (c, i, j)),
        out_shape=jax.ShapeDtypeStruct((3, t, d), F32),
        scratch_shapes=[pltpu.VMEM((tm, d), MXU_DTYPE)],
        compiler_params=_params("parallel", "parallel", "arbitrary"),
        name="rwkv_rkv",
    )(h, h_prev, mu.reshape(-1, 1, d), w_rkv)


def _softplus(y):
    return jnp.maximum(y, 0.0) + jnp.log1p(jnp.exp(-jnp.abs(y)))


def _rwkv_lora_kernel(h_ref, p_ref, mu_ref, w0_ref, w1_ref, w2_ref, a0_ref, a1_ref, a2_ref,
                      g1_ref, g2_ref, lw_ref, a_ref, gate_ref):
    h = h_ref[...]
    xx = p_ref[...] - h
    lerp = lambda c: h + xx * mu_ref[c]
    z = w0_ref[...] + _mm(jnp.tanh(_mm(lerp(0), w1_ref[...])), w2_ref[...])
    w_log = -_softplus(-z) - 0.5
    lw_ref[...] = -jnp.exp(w_log)
    a_ref[...] = _sigmoid(a0_ref[...] + _mm(_mm(lerp(1), a1_ref[...]), a2_ref[...]))
    gate_ref[...] = _mm(_sigmoid(_mm(lerp(2), g1_ref[...])), g2_ref[...])


def _rwkv_lora(h, h_prev, mu_wag, w0, w1, w2, a0, a1, a2, g1, g2):
    t, d = h.shape
    tm = _tile(t, 256)
    row = pl.BlockSpec((tm, d), lambda i: (i, 0))
    full = lambda a: pl.BlockSpec(a.shape, lambda i: (0,) * a.ndim)
    consts = [mu_wag.reshape(3, 1, d), w0.reshape(1, d), w1, w2, a0.reshape(1, d), a1, a2, g1, g2]
    return pl.pallas_call(
        _rwkv_lora_kernel,
        grid=(t // tm,),
        in_specs=[row, row] + [full(c) for c in consts],
        out_specs=[row, row, row],
        out_shape=[jax.ShapeDtypeStruct((t, d), F32)] * 3,
        compiler_params=_params("parallel"),
        name="rwkv_lora",
    )(h, h_prev, *consts)


def _rwkv_kernel(r_ref, k_ref, v_ref, lw_ref, a_ref, gate_ref, kkw_ref, kaw_ref, rkw_ref,
                 lng_ref, lnb_ref, s0_ref, o_ref, sout_ref, s_ref, *, c_rows, nchunk, npb, ngrp, nstream):
    i = pl.program_id(2)
    c2 = 2 * c_rows

    @pl.when(i == 0)
    def _():
        s_ref[...] = s0_ref[...]

    lane = lax.broadcasted_iota(jnp.int32, (c_rows, LANES), 1)
    head_a = lane < RWKV_HEAD
    si = lax.broadcasted_iota(jnp.int32, (c2, c2), 0)
    sj = lax.broadcasted_iota(jnp.int32, (c2, c2), 1)
    same_blk = (si // c_rows) == (sj // c_rows)
    strict = jnp.logical_and(same_blk, (sj % c_rows) < (si % c_rows))
    incl = jnp.logical_and(same_blk, (sj % c_rows) <= (si % c_rows))
    inv_n = 1.0 / RWKV_HEAD

    def stack(x):
        return jnp.concatenate([jnp.where(head_a, x, 0.0), jnp.where(head_a, 0.0, x)], axis=0)

    def head_sum(x):
        sa = jnp.sum(jnp.where(head_a, x, 0.0), axis=-1, keepdims=True)
        sb = jnp.sum(jnp.where(head_a, 0.0, x), axis=-1, keepdims=True)
        return jnp.where(head_a, sa, sb)

    def prepare(rows, lanes):
        r, kr, v = r_ref[rows, lanes], k_ref[rows, lanes], v_ref[rows, lanes]
        lw, a = lw_ref[rows, lanes], a_ref[rows, lanes]
        kkw, kaw = kkw_ref[:, lanes], kaw_ref[:, lanes]
        kk = kr * kkw
        ss = head_sum(kk * kk)
        cl = _cumsum_rows(lw)
        yield
        kk = kk * lax.rsqrt(jnp.maximum(ss, 1e-24))
        k = kr * (1.0 + (a - 1.0) * kaw)
        e_neg = jnp.exp(-cl)
        al = stack(-kk * jnp.exp(cl - lw))
        rt = stack(r * jnp.exp(cl))
        bg = stack(a * kk * e_neg)
        rhs = jnp.concatenate([bg, stack(k * e_neg)], axis=0)
        aa = _mm_nt(jnp.concatenate([al, rt], axis=0), rhs)
        yield
        n_ab = jnp.where(strict, aa[:c2, :c2], 0.0)
        a_ak = jnp.where(strict, aa[:c2, c2:], 0.0)
        a_rb = jnp.where(incl, aa[c2:, :c2], 0.0)
        a_rk = jnp.where(incl, aa[c2:, c2:], 0.0)
        vs = stack(v)
        x = jnp.concatenate([al, _mm(a_ak, vs)], axis=1)
        yield
        pw = n_ab
        x = x + _mm(pw, x)
        yield
        for _ in range(int(math.log2(c_rows)) - 1):
            pw = _mm(pw, pw)
            yield
            x = x + _mm(pw, x)
            yield
        w, u0 = x[:, :LANES], x[:, LANES:]
        y = _mm(a_rb, x)
        yield
        ro = rt + y[:, :LANES]
        o0 = y[:, LANES:] + _mm(a_rk, vs)
        yield
        p = _mm_tn(w, bg)
        yield
        q = _mm_tn(jnp.concatenate([u0, vs], axis=0), rhs)
        yield
        gam = jnp.exp(cl[c_rows - 1:c_rows, :])
        bonus = head_sum(r * k * rkw_ref[:, lanes]) * v
        return ro, o0, p, q, gam, bonus

    def finish(rows, lanes, o2, bonus):
        o = o2[:c_rows] + o2[c_rows:]
        mean = head_sum(o) * inv_n
        dlt = o - mean
        var = head_sum(dlt * dlt) * inv_n
        on =dlt * lax.rsqrt(var + RWKV_GN_EPS) * lng_ref[:, lanes] + lnb_ref[:, lanes]
        o_ref[rows, lanes] = (on + bonus) * gate_ref[rows, lanes]

    def group(g, carry):
        r0 = pl.multiple_of(g * (ngrp * c_rows), ngrp * c_rows)
        lanes = [slice(pi * LANES, (pi + 1) * LANES) for pi in range(npb)]
        rows = {(sn, c): pl.ds(sn * (nchunk * c_rows) + r0 + c * c_rows, c_rows)
                for sn in range(nstream) for c in range(ngrp)}
        units = [(sn, c, pi) for c in range(ngrp) for sn in range(nstream) for pi in range(npb)]
        prep = dict(zip(units, _lockstep([prepare(rows[sn, c], lanes[pi]) for sn, c, pi in units])))
        s = {(sn, pi): s_ref[sn, pi] for sn in range(nstream) for pi in range(npb)}
        for sn, c, pi in units:
            ro, o0, p, q, gam, bonus = prep[sn, c, pi]
            finish(rows[sn, c], lanes[pi], _mm_nt(ro, s[sn, pi]) + o0, bonus)
            s[sn, pi] = (s[sn, pi] + _mm(s[sn, pi], p) + q) * gam
        for (sn, pi), val in s.items():
            s_ref[sn, pi] = val
        return carry

    lax.fori_loop(0, nchunk // ngrp, group, 0)

    @pl.when(i == pl.num_programs(2) - 1)
    def _():
        sout_ref[...] = s_ref[...]


def _rwkv_recurrence(rkv, lw, a, gate, kkw, kaw, rkw, lng, lnb, s0_blk, b, t):
    d = lw.shape[1]
    npair = d // LANES
    c_rows = min(CHUNK, t)
    tb = _tile(t, 512)
    nb = t // tb
    nchunk = tb // c_rows
    ns = _streams_per_block(b, t, tb)
    npb = 2 if npair % 2 == 0 else 1
    ngrp = _tile(nchunk, RWKV_CHUNKS_IN_FLIGHT)
    bw = npb * LANES
    kern = functools.partial(_rwkv_kernel, c_rows=c_rows, nchunk=nchunk, npb=npb, ngrp=ngrp, nstream=ns)
    rowblk = pl.BlockSpec((ns * tb, bw), lambda bi, p, i: (bi * nb + i, p))
    rkvblk = lambda c: pl.BlockSpec((None, ns * tb, bw), lambda bi, p, i: (c, bi * nb + i, p))
    vec = pl.BlockSpec((1, bw), lambda bi, p, i: (0, p))
    st = pl.BlockSpec((ns, npb, LANES, LANES), lambda bi, p, i: (bi, p, 0, 0))
    return pl.pallas_call(
        kern,
        grid=(b // ns, npair // npb, nb),
        in_specs=[rkvblk(0), rkvblk(1), rkvblk(2), rowblk, rowblk, rowblk, vec, vec, vec, vec, vec, st],
        out_specs=[rowblk, st],
        out_shape=[jax.ShapeDtypeStruct((b * t, d), F32),
                   jax.ShapeDtypeStruct((b, npair, LANES, LANES), F32)],
        scratch_shapes=[pltpu.VMEM((ns, npb, LANES, LANES), F32)],
        compiler_params=_params("parallel", "parallel", "arbitrary"),
        name="rwkv_recurrence",
    )(rkv, rkv, rkv, lw, a, gate, kkw.reshape(1, d), kaw.reshape(1, d), rkw.reshape(1, d),
      lng.reshape(1, d), lnb.reshape(1, d), s0_blk)


def _to_blockdiag(s):
    b, h, n, _ = s.shape
    s = s.reshape(b, h // 2, 2, n, n)
    z = jnp.zeros_like(s[:, :, 0])
    top = jnp.concatenate([s[:, :, 0], z], axis=-1)
    bot = jnp.concatenate([z, s[:, :, 1]], axis=-1)
    return jnp.concatenate([top, bot], axis=-2)


def _from_blockdiag(sb):
    b, p, n2, _ = sb.shape
    n = n2 // 2
    return jnp.stack([sb[:, :, :n, :n], sb[:, :, n:, n:]], axis=2).reshape(b, 2 * p, n, n)


def _trunk(x3, hgrn_s0, k_cache, v_cache, wkv_s0, shift0, w, lower):
    b, t, d = x3.shape
    x = x3.reshape(b * t, d)
    depth = w['norm_g'].shape[0]
    hgrn_out, k_out, v_out, wkv_out, shift_out = [], [], [], [], []
    xn = _norm(x, w['norm_g'][0, 0], MXU_DTYPE)
    for layer in range(depth):
        kind, j = layer % 3, layer // 3
        g = w['norm_g'][layer]
        last = layer == depth - 1
        ffn = w['ffn'][2 * layer:2 * layer + 3] + [None]
        x, n = _ffn(x, xn, ffn[0], ffn[1], g[1], F32 if kind == 2 else MXU_DTYPE)
        if kind == 0:
            proj = _proj(n, w['hgrn_w_in'][j])
            o, s_fin = _hgrn_recurrence(proj, lower[layer], hgrn_s0[j], b, t)
            hgrn_out.append(s_fin)
            x, xn = _gated_proj_res(x, o, proj, 3, w['hgrn_norm_g'][j], w['hgrn_w_out'][j], g[2], MXU_DTYPE)
        elif kind == 1:
            qkv = _proj(n, w['attn_w_qkv'][j])
            rel_bias = w['attn_rel_bias'][j]
            nh = rel_bias.shape[0]
            if k_cache is None:
                past = PREV_CHUNKS * CHUNK
                o = _attn_prompt(qkv, _rel_bias(rel_bias, CHUNK, BAND, past), b, t)
                kept = min(past, t)
            else:
                rows = k_cache.shape[2]
                bias = _rel_bias(rel_bias, t, rows + t, rows)
                o = _attn_cached(qkv, k_cache[j].reshape(b, rows, d), v_cache[j].reshape(b, rows, d),
                                 bias[:, :, :rows], bias[:, :, rows:], b, t)
                kept = t
            qkv3 = qkv.reshape(b, t, 3 * d)
            keep = lambda c: qkv3[:, t - kept:, c * d:(c + 1) * d].reshape(b, kept, nh, ATTN_HEAD_DIM)
            k_out.append(keep(1))
            v_out.append(keep(2))
            x, xn = _proj_res(x, o, w['attn_w_out'][j], g[2], MXU_DTYPE)
        else:
            h = n
            h3 = h.reshape(b, t, d)
            h_prev = jnp.concatenate([shift0[j].astype(F32), h3[:, :-1]], axis=1).reshape(b * t, d)
            mu = w['rwkv_mu'][j]
            rkv = _rwkv_rkv(h, h_prev, mu[:3], w['rwkv_w_rkv'][j])
            lw, a, gate = _rwkv_lora(h, h_prev, mu[3:], w['rwkv_w0'][j], w['rwkv_w1'][j], w['rwkv_w2'][j],
                                     w['rwkv_a0'][j], w['rwkv_a1'][j], w['rwkv_a2'][j],
                                     w['rwkv_g1'][j], w['rwkv_g2'][j])
            o, s_blk = _rwkv_recurrence(rkv, lw, a, gate, w['rwkv_k_k'][j], w['rwkv_k_a'][j],
                                        w['rwkv_r_k'][j].reshape(-1), w['rwkv_ln_g'][j], w['rwkv_ln_b'][j],
                                        _to_blockdiag(wkv_s0[j]), b, t)
            shift_out.append(h3[:, -1:])
            wkv_out.append(_from_blockdiag(s_blk))
            x, xn = _proj_res(x, o, w['rwkv_w_out'][j], g[2], MXU_DTYPE)
        g_next = w['final_norm_g'] if last else w['norm_g'][layer + 1, 0]
        x, xn = _ffn(x, xn, ffn[1], ffn[2], g_next, F32 if last else MXU_DTYPE, keep_x=not last)
    y = xn.reshape(b, t, d)
    return (y,jnp.stack(hgrn_out), jnp.stack(k_out), jnp.stack(v_out),
            jnp.stack(wkv_out), jnp.stack(shift_out))


def kernel(x_prompt, x_sample, state_hgrn, cache_k_band, cache_v_band, state_wkv, state_shift,
           norm_g, final_norm_g, ffn_w_gate_up, ffn_w_down,
           hgrn_w_in, hgrn_lb_logits, hgrn_norm_g, hgrn_w_out,
           attn_w_qkv, attn_rel_bias, attn_w_out,
           rwkv_mu, rwkv_w_rkv, rwkv_w0, rwkv_w1, rwkv_w2, rwkv_a0, rwkv_a1, rwkv_a2,
           rwkv_g1, rwkv_g2, rwkv_k_k, rwkv_k_a, rwkv_r_k, rwkv_ln_g, rwkv_ln_b, rwkv_w_out):
    depth = norm_g.shape[0]
    cast = lambda a: a.astype(MXU_DTYPE)
    ffn = [dict(gu32=ffn_w_gate_up[l, i], d32=ffn_w_down[l, i], gate=None, up=None, down=None)
           for l in range(depth) for i in range(2)]
    w = dict(norm_g=norm_g, final_norm_g=final_norm_g, ffn=ffn,
             hgrn_w_in=cast(hgrn_w_in), hgrn_norm_g=hgrn_norm_g, hgrn_w_out=cast(hgrn_w_out),
             attn_w_qkv=cast(attn_w_qkv), attn_rel_bias=attn_rel_bias, attn_w_out=cast(attn_w_out),
             rwkv_mu=rwkv_mu, rwkv_w_rkv=cast(rwkv_w_rkv), rwkv_w0=rwkv_w0, rwkv_w1=cast(rwkv_w1),
             rwkv_w2=cast(rwkv_w2), rwkv_a0=rwkv_a0, rwkv_a1=cast(rwkv_a1), rwkv_a2=cast(rwkv_a2),
             rwkv_g1=cast(rwkv_g1), rwkv_g2=cast(rwkv_g2), rwkv_k_k=rwkv_k_k, rwkv_k_a=rwkv_k_a,
             rwkv_r_k=rwkv_r_k, rwkv_ln_g=rwkv_ln_g, rwkv_ln_b=rwkv_ln_b, rwkv_w_out=cast(rwkv_w_out))
    probs = jax.nn.softmax(hgrn_lb_logits.astype(F32), axis=0)
    lower = jnp.cumsum(probs, axis=0) - probs[0]

    b = x_prompt.shape[0]
    d = x_prompt.shape[2]
    n_a, n_c = state_hgrn.shape[0], state_wkv.shape[0]
    hgrn0 = jnp.zeros((n_a, b) + state_hgrn.shape[2:], F32)
    wkv0 = jnp.zeros((n_c, b) + state_wkv.shape[2:], F32)
    shift0 = jnp.zeros((n_c, b, 1, d), F32)
    y_p, hgrn_p, k_p, v_p, wkv_p, shift_p = _trunk(x_prompt, hgrn0, None, None, wkv0, shift0, w, lower)
    y_s, hgrn_s, k_s, v_s, wkv_s, shift_s = _trunk(x_sample, state_hgrn, cache_k_band, cache_v_band,
                                                   state_wkv, state_shift, w, lower)
    return (y_p, y_s, hgrn_p, hgrn_s, k_p, v_p, k_s, v_s, wkv_p, wkv_s, shift_p, shift_s)
```

```python
import functools
import math

import numpy as np
import jax
import jax.numpy as jnp
from jax import lax
from jax.experimental import pallas as pl
from jax.experimental.pallas import tpu as pltpu

F32 = jnp.float32
MXU_DTYPE = jnp.bfloat16

LANES = 128
CHUNK = 64
PREV_CHUNKS = 8
BAND = (PREV_CHUNKS + 1) * CHUNK
REL_CLIP = 128
NORM_EPS = 1e-6
RWKV_GN_EPS = 64e-5
HGRN_HEAD = 128
HGRN_BLOCKS_IN_FLIGHT = 4
HGRN_FINE_LEVELS = (8, 4)
ATTN_HEAD_DIM = 64
ATTN_CHUNKS_IN_FLIGHT = 4
RWKV_HEAD = 64
RWKV_CHUNKS_IN_FLIGHT = 8
CAST_MAX_ROWS = 256
STREAM_ROWS_PER_STEP = 128
VMEM_LIMIT = 60 * 1024 * 1024


def _params(*sem):
    return pltpu.CompilerParams(dimension_semantics=sem, vmem_limit_bytes=VMEM_LIMIT)


def _tile(n, pref):
    if n <= pref:
        return n
    t = pref
    while n % t:
        t //= 2
    return t


def _streams_per_block(b, t, tb):
    if tb < t:
        return 1
    ns = b
    while ns > 1 and ns * t > STREAM_ROWS_PER_STEP:
        ns //= 2
    return ns if b % ns == 0 else 1


def _mm(a, b):
    return jnp.dot(a.astype(MXU_DTYPE), b.astype(MXU_DTYPE), preferred_element_type=F32)


def _mm_nt(a, b):
    return lax.dot_general(a.astype(MXU_DTYPE), b.astype(MXU_DTYPE),
                           (((1,), (1,)), ((), ())), preferred_element_type=F32)


def _mm_tn(a, b):
    return lax.dot_general(a.astype(MXU_DTYPE), b.astype(MXU_DTYPE),
                           (((0,), (0,)), ((), ())), preferred_element_type=F32)


def _split3(x):
    hi = x.astype(jnp.bfloat16)
    r1 = x - hi.astype(F32)
    mid = r1.astype(jnp.bfloat16)
    lo = (r1 - mid.astype(F32)).astype(jnp.bfloat16)
    return hi, mid, lo


def _sel_mm(sel, x):
    sel = sel.astype(jnp.bfloat16)
    hi, mid, lo = _split3(x)
    d = lambda p: jnp.dot(sel, p, preferred_element_type=F32)
    return d(hi) + (d(mid) + d(lo))


def _cumsum_rows(x):
    row = lax.broadcasted_iota(jnp.int32, x.shape, 0)
    shift = 1
    while shift < x.shape[0]:
        x = x + jnp.where(row >= shift, pltpu.roll(x, shift, axis=0), 0.0)
        shift *= 2
    return x


def _lockstep(gens):
    out = [None] * len(gens)
    live = list(range(len(gens)))
    while live:
        for n in list(live):
            try:
                next(gens[n])
            except StopIteration as stop:
                out[n] = stop.value
                live.remove(n)
    return out


def _rms(x):
    return x * lax.rsqrt(jnp.mean(x * x, axis=-1, keepdims=True) + NORM_EPS)


def _sigmoid(x):
    return 1.0 / (1.0 + jnp.exp(-x))


def _silu(x):
    return x * _sigmoid(x)


def _ffn_up_kernel(xn_ref, wg_ref, wu_ref, *refs, tail, cast_down, cast_next):
    refs = list(refs)
    d32_ref = refs.pop(0) if cast_down else None
    gu32_ref = refs.pop(0) if cast_next else None
    h_ref = refs.pop(0)
    if cast_down:
        refs.pop(0)[...] = d32_ref[...].astype(MXU_DTYPE)
    if cast_next:
        f_next = gu32_ref.shape[1] // 2
        refs.pop(0)[...] = gu32_ref[:, :f_next].astype(MXU_DTYPE)
        refs.pop(0)[...] = gu32_ref[:, f_next:].astype(MXU_DTYPE)
    j = pl.program_id(1)

    def columns(width):
        tm = xn_ref.shape[0]
        sub = _tile(tm, 512)
        for r in range(0, tm, sub):
            xn = xn_ref[r:r + sub, :]
            gate = jnp.dot(xn, wg_ref[:, :width], preferred_element_type=F32)
            up = jnp.dot(xn, wu_ref[:, :width], preferred_element_type=F32)
            h_ref[r:r + sub, :width] = (_silu(gate) * up).astype(h_ref.dtype)

    last = pl.num_programs(1) - 1
    pl.when(j < last)(lambda: columns(h_ref.shape[1]))
    pl.when(j == last)(lambda: columns(tail))


def _cast_rows(nrows, steps):
    rb = -(-nrows // (16 * steps)) * 16
    return rb if rb <= CAST_MAX_ROWS else None


def _ffn(x, xn, wk, w_next, g_next, n_dtype, keep_x=True):
    t, d = x.shape
    f = wk['d_all'].shape[-2]
    tf = min(f, 1024)
    nf = pl.cdiv(f, tf)
    tm = _tile(t, 2048)
    steps = (t // tm) * nf
    rb_d = _cast_rows(f, steps) if wk['down'] is None else None
    rb_n = _cast_rows(d, steps) if w_next is not None and w_next['gate'] is None else None
    if wk['gate'] is None:
        wk['gate'] = wk['gu_all'][wk['idx']][:, :f].astype(MXU_DTYPE)
        wk['up'] = wk['gu_all'][wk['idx']][:, f:].astype(MXU_DTYPE)
    rows = lambda rb, n: (lambda i, j: (jnp.minimum(i * nf + j, pl.cdiv(n, rb) - 1), 0))
    stacked = lambda rb, n, idx: (lambda i, j: idx + rows(rb, n)(i, j))
    ins = [xn, wk['gate'], wk['up']]
    in_specs = [pl.BlockSpec((tm, d), lambda i, j: (i, 0)),
                pl.BlockSpec((d, tf), lambda i, j: (0, j)),
                pl.BlockSpec((d, tf), lambda i, j: (0, j))]
    out_specs = [pl.BlockSpec((tm, tf), lambda i, j: (i, j))]
    out_shape = [jax.ShapeDtypeStruct((t, f), MXU_DTYPE)]
    if rb_d:
        ins.append(wk['d_all'])
        in_specs.append(pl.BlockSpec((None, None, rb_d, d), stacked(rb_d, f, wk['idx'])))
        out_specs.append(pl.BlockSpec((rb_d, d), rows(rb_d, f)))
        out_shape.append(jax.ShapeDtypeStruct((f, d), MXU_DTYPE))
    if rb_n:
        f2 = w_next['gu_all'].shape[-1]
        ins.append(w_next['gu_all'])
        in_specs.append(pl.BlockSpec((None, None, rb_n, f2), stacked(rb_n, d, w_next['idx'])))
        out_specs += [pl.BlockSpec((rb_n, f2 // 2), rows(rb_n, d))] * 2
        out_shape += [jax.ShapeDtypeStruct((d, f2 // 2), MXU_DTYPE)] * 2
    outs = pl.pallas_call(
        functools.partial(_ffn_up_kernel, tail=f - (nf - 1) * tf, cast_down=bool(rb_d), cast_next=bool(rb_n)),
        grid=(t // tm, nf),
        in_specs=in_specs,
        out_specs=out_specs,
        out_shape=out_shape,
        compiler_params=_params("arbitrary", "arbitrary"),
        name="ffn_up",
    )(*ins)
    h = outs[0]
    if rb_d:
        wk['down'] = outs[1]
    elif wk['down'] is None:
        wk['down'] = wk['d_all'][wk['idx']].astype(MXU_DTYPE)
    if rb_n:
        w_next['gate'], w_next['up'] = outs[-2:]
    return _proj_res(x, h, wk['down'], g_next, n_dtype, scale=0.5, keep_x=keep_x)


def _proj_kernel(xn_ref, w_ref, o_ref):
    o_ref[...] = jnp.dot(xn_ref[...], w_ref[...], preferred_element_type=F32)


def _proj(xn, w):
    t, d = xn.shape
    n = w.shape[1]
    tm, tn = _tile(t, 1024), _tile(n, 2048)
    return pl.pallas_call(
        _proj_kernel,
        grid=(t // tm, n // tn),
        in_specs=[
            pl.BlockSpec((tm, d), lambda i, j: (i, 0)),
            pl.BlockSpec((d, tn), lambda i, j: (0, j)),
        ],
        out_specs=pl.BlockSpec((tm, tn), lambda i, j: (i, j)),
        out_shape=jax.ShapeDtypeStruct((t, n), F32),
        compiler_params=_params("parallel", "parallel"),
        name="proj",
    )(xn, w)


def _norm_kernel(x_ref, g_ref, o_ref):
    o_ref[...] = (_rms(x_ref[...]) * g_ref[...]).astype(o_ref.dtype)


def _norm(x, g, dtype):
    t, d = x.shape
    tm = _tile(t, 512)
    return pl.pallas_call(
        _norm_kernel,
        grid=(t // tm,),
        in_specs=[pl.BlockSpec((tm, d), lambda i: (i, 0)), pl.BlockSpec((1, d), lambda i: (0, 0))],
        out_specs=pl.BlockSpec((tm, d), lambda i: (i, 0)),
        out_shape=jax.ShapeDtypeStruct((t, d), dtype),
        compiler_params=_params("parallel"),
        name="norm",
    )(x, g.reshape(1, d))


def _emit(y, gn_ref, o_ref, n_ref, rows=slice(None)):
    if o_ref is not None:
        o_ref[rows, :] = y
    n_ref[rows, :] = (_rms(y) * gn_ref[...]).astype(n_ref.dtype)


def _proj_res_kernel(x_ref, a_ref, w_ref, gn_ref, *out_refs, scale):
    o_ref, n_ref = out_refs if len(out_refs) == 2 else (None, out_refs[0])
    y = _mm(a_ref[...], w_ref[...])
    _emit(x_ref[...] + (y if scale == 1.0 else scale * y), gn_ref, o_ref, n_ref)


def _gated_proj_res_kernel(x_ref, a_ref, z_ref, g_ref, w_ref, gn_ref, o_ref, n_ref, *, sub):
    for r in range(0, x_ref.shape[0], sub):
        rows = slice(r, r + sub)
        an = _rms(a_ref[rows, :]) * g_ref[...] * _silu(z_ref[rows, :])
        _emit(x_ref[rows, :] + _mm(an, w_ref[...]), gn_ref, o_ref, n_ref, rows)


def _resident(shape):
    return pl.BlockSpec(shape, lambda i: (0,) * len(shape), pipeline_mode=pl.Buffered(1))


def _proj_res(x, a, w, g_next, n_dtype, scale=1.0, keep_x=True):
    t, d = x.shape
    k = a.shape[1]
    tm = _tile(t, 512)
    row = pl.BlockSpec((tm, d), lambda i: (i, 0))
    outs = pl.pallas_call(
        functools.partial(_proj_res_kernel, scale=scale),
        grid=(t // tm,),
        in_specs=[row, pl.BlockSpec((tm, k), lambda i: (i, 0)), _resident((k, d)), _resident((1, d))],
        out_specs=[row] * (1 + keep_x),
        out_shape=[jax.ShapeDtypeStruct((t, d), F32)] * keep_x + [jax.ShapeDtypeStruct((t, d), n_dtype)],
        compiler_params=_params("parallel"),
        name="proj_res",
    )(x, a, w, g_next.reshape(1, d))
    return tuple(outs) if keep_x else (None, outs[0])


def _gated_proj_res(x, a, z_src, z_col, g, w, g_next, n_dtype):
    t, d = x.shape
    tm = _tile(t, 512)
    row = lambda col: pl.BlockSpec((tm, d), lambda i: (i, col))
    return pl.pallas_call(
        functools.partial(_gated_proj_res_kernel, sub=_tile(tm, 128)),
        grid=(t // tm,),
        in_specs=[row(0), row(0), row(z_col), _resident((1, d)), _resident((d, d)), _resident((1, d))],
        out_specs=[row(0), row(0)],
        out_shape=[jax.ShapeDtypeStruct((t, d), F32), jax.ShapeDtypeStruct((t, d), n_dtype)],
        compiler_params=_params("parallel"),
        name="gated_proj_res",
    )(x, a, z_src, g.reshape(1, d), w, g_next.reshape(1, d))


def _hgrn_levels(rows):
    t = np.arange(rows)[:, None]
    r = np.arange(rows)[None, :]
    mats = [r <= t]
    for b in HGRN_FINE_LEVELS:
        mid = (t // b) * b + b // 2 - 1
        upper = (t % b) >= b // 2
        mats.append(np.where(upper, (r > mid) & (r <= t), (r > t) & (r <= mid)))
    return np.concatenate(mats, axis=0).astype(np.float32)


def _hgrn_kernel(q_ref, f_ref, v_ref, lb_ref, lvl_ref, s0_ref, o_ref, sout_ref, st_ref, *,
                 rows, nblk, nstream):
    i = pl.program_id(2)

    @pl.when(i == 0)
    def _():
        for s in range(nstream):
            st_ref[s] = s0_ref[s, 0].T

    lb = lb_ref[...]
    lvl = lvl_ref[...]
    t_id = lax.broadcasted_iota(jnp.int32, (rows, HGRN_HEAD), 0)
    ti = lax.broadcasted_iota(jnp.int32, (rows, rows), 0)
    si = lax.broadcasted_iota(jnp.int32, (rows, rows), 1)
    scale = HGRN_HEAD ** -0.5

    def level(scores, b, qe, ke):
        upper = (t_id & (b - 1)) >= b // 2
        s_l = _mm_nt(jnp.where(upper, qe, 0.0), jnp.where(upper, 0.0, ke))
        if b < rows:
            sh = int(math.log2(b))
            s_l = jnp.where((ti >> sh) == (si >> sh), s_l, 0.0)
        return scores + s_l

    def block(sl):
        q = _silu(q_ref[sl, :]) * scale
        fz = f_ref[sl, :]
        v = v_ref[sl, :]
        f = lb + (1.0 - lb) * _sigmoid(fz)
        k = (1.0 - lb) * _sigmoid(-fz)
        sums = _sel_mm(lvl, jnp.log(f))
        yield
        g = sums[:rows]
        scores = jnp.where(ti == si, _mm_nt(q, k), 0.0)
        yield
        scores = level(scores, 2, q * f, k)
        yield
        for n, b in enumerate(HGRN_FINE_LEVELS):
            if b <= rows:
                e = jnp.exp(sums[(n + 1) * rows:(n + 2) * rows])
                scores = level(scores, b, q * e, k * e)
                yield
        b = 2 * HGRN_FINE_LEVELS[0]
        while b <= rows:
            g_mid = jnp.concatenate(
                [jnp.broadcast_to(g[m:m + 1, :], (b, HGRN_HEAD)) for m in range(b // 2 - 1, rows, b)], axis=0)
            upper = (t_id & (b - 1)) >= b // 2
            e = jnp.exp(jnp.where(upper, g - g_mid, g_mid - g))
            scores = level(scores, b, q * e, k * e)
            yield
            b *= 2
        g_last = g[rows - 1:rows, :]
        o_intra = _mm(scores, v)
        yield
        kv = _mm_tn(v, k * jnp.exp(g_last - g))
        return o_intra, q * jnp.exp(g), jnp.exp(g_last), kv

    units = [(s, slice((s * nblk + u) * rows, (s * nblk + u + 1) * rows))
             for s in range(nstream) for u in range(nblk)]
    st = [st_ref[s] for s in range(nstream)]
    for u0 in range(0, len(units), HGRN_BLOCKS_IN_FLIGHT):
        group = units[u0:u0 + HGRN_BLOCKS_IN_FLIGHT]
        blocks = _lockstep([block(sl) for _, sl in group])
        for (s, sl), (o_intra, qg, dec, kv) in zip(group, blocks):
            o_ref[sl, :] = o_intra + _mm_nt(qg, st[s])
            st[s] = st[s] * dec + kv
    for s in range(nstream):
        st_ref[s] = st[s]

    @pl.when(i == pl.num_programs(2) - 1)
    def _():
        for s in range(nstream):
            sout_ref[s, 0] = st[s].T


def _hgrn_recurrence(proj, lb, s0, b, t):
    d = proj.shape[1] // 4
    h = d // HGRN_HEAD
    tb = _tile(t, 1024)
    nb = t // tb
    rows = _tile(tb, 128)
    ns = _streams_per_block(b, t, tb)
    lvl = jnp.asarray(_hgrn_levels(rows), dtype=jnp.bfloat16)
    kern = functools.partial(_hgrn_kernel, rows=rows, nblk=tb // rows, nstream=ns)
    row = lambda bi, hi, i: bi * nb + i
    blk = lambda col: pl.BlockSpec((ns * tb, HGRN_HEAD), lambda bi, hi, i: (row(bi, hi, i), col * h + hi))
    state = pl.BlockSpec((ns, 1, HGRN_HEAD, HGRN_HEAD), lambda bi, hi, i: (bi, hi, 0, 0))
    return pl.pallas_call(
        kern,
        grid=(b // ns, h, nb),
        in_specs=[
            blk(0), blk(1), blk(2),
            pl.BlockSpec((1, HGRN_HEAD), lambda bi, hi, i: (0, hi)),
            pl.BlockSpec(lvl.shape, lambda bi, hi, i: (0, 0)),
            state,
        ],
        out_specs=[blk(0), state],
        out_shape=[jax.ShapeDtypeStruct((b * t, d), F32),
                   jax.ShapeDtypeStruct((b, h, HGRN_HEAD, HGRN_HEAD), F32)],
        scratch_shapes=[pltpu.VMEM((ns, HGRN_HEAD, HGRN_HEAD), F32)],
        compiler_params=_params("parallel", "parallel", "arbitrary"),
        name="hgrn_recurrence",
    )(proj, proj, proj, lb.reshape(1, d), lvl, s0)


def _softmax_pv(parts):
    m = functools.reduce(jnp.maximum, [jnp.max(s, axis=-1, keepdims=True) for s, _ in parts])
    es = [jnp.exp(s - m) for s, _ in parts]
    den = functools.reduce(jnp.add, [jnp.sum(e, axis=-1, keepdims=True) for e in es])
    yield
    pv = functools.reduce(jnp.add, [_mm(e, v) for e, (_, v) in zip(es, parts)])
    yield
    return pv / den


def _attn_prompt_kernel(q_ref, kp_ref, kc_ref, vp_ref, vc_ref, bias_ref, o_ref, k_scr, v_scr, *, tq):
    i = pl.program_id(2)
    k_scr[0:tq, :] = kp_ref[...].astype(k_scr.dtype)
    k_scr[tq:2 * tq, :] = kc_ref[...].astype(k_scr.dtype)
    v_scr[0:tq, :] = vp_ref[...].astype(v_scr.dtype)
    v_scr[tq:2 * tq, :] = vc_ref[...].astype(v_scr.dtype)
    lane = lax.broadcasted_iota(jnp.int32, (CHUNK, LANES), 1)
    head_a = lane < ATTN_HEAD_DIM
    col = lax.broadcasted_iota(jnp.int32, (CHUNK, BAND), 1)
    scale = ATTN_HEAD_DIM ** -0.5
    past = PREV_CHUNKS * CHUNK

    def unit(j, hd, masked):
        q = q_ref[j * CHUNK:(j + 1) * CHUNK, :] * scale
        w0 = tq - past + j * CHUNK
        sel = head_a if hd == 0 else jnp.logical_not(head_a)
        s = _mm_nt(jnp.where(sel, q, 0.0), k_scr[w0:w0 + BAND, :]) + bias_ref[hd]
        yield
        if masked:
            s = jnp.where(col + ((i - 1) * tq + w0) >= 0, s, -jnp.inf)
        return (yield from _softmax_pv([(s, v_scr[w0:w0 + BAND, :])]))

    def run(masked):
        for j0 in range(0, tq // CHUNK, ATTN_CHUNKS_IN_FLIGHT):
            js = range(j0, min(j0 + ATTN_CHUNKS_IN_FLIGHT, tq // CHUNK))
            outs = _lockstep([unit(j, hd, masked) for j in js for hd in range(2)])
            for n, j in enumerate(js):
                o_ref[j * CHUNK:(j + 1) * CHUNK, :] = jnp.where(head_a, outs[2 * n], outs[2 * n + 1])

    @pl.when(i == 0)
    def _():
        run(True)

    @pl.when(i > 0)
    def _():
        run(False)


def _attn_prompt(qkv, bias, b, t):
    d = qkv.shape[1] // 3
    npair = d // LANES
    tq = _tile(t, 512)
    nq = t // tq
    assert tq >= PREV_CHUNKS * CHUNK and tq % CHUNK == 0
    cur = lambda bi, i: bi * nq + i
    prev = lambda bi, i: bi * nq + jnp.maximum(i - 1, 0)
    return pl.pallas_call(
        functools.partial(_attn_prompt_kernel, tq=tq),
        grid=(b, npair, nq),
        in_specs=[
            pl.BlockSpec((tq, LANES), lambda bi, p, i: (cur(bi, i), p)),
            pl.BlockSpec((tq, LANES), lambda bi, p, i: (prev(bi, i), npair + p)),
            pl.BlockSpec((tq, LANES), lambda bi, p, i: (cur(bi, i), npair + p)),
            pl.BlockSpec((tq, LANES), lambda bi, p, i: (prev(bi, i), 2 * npair + p)),
            pl.BlockSpec((tq, LANES), lambda bi, p, i: (cur(bi, i), 2 * npair + p)),
            pl.BlockSpec((2, CHUNK, BAND), lambda bi, p, i: (p, 0, 0)),
        ],
        out_specs=pl.BlockSpec((tq, LANES), lambda bi, p, i: (cur(bi, i), p)),
        out_shape=jax.ShapeDtypeStruct((b * t, d), F32),
        scratch_shapes=[pltpu.VMEM((2 * tq, LANES), MXU_DTYPE), pltpu.VMEM((2 * tq, LANES), MXU_DTYPE)],
        compiler_params=_params("parallel", "parallel", "arbitrary"),
        name="attn_prompt",
    )(qkv, qkv, qkv, qkv, qkv, bias)


def _attn_cached_kernel(q_ref, kn_ref, vn_ref, kc_ref, vc_ref, bc_ref, bn_ref, o_ref, *, t, nstream):
    lane = lax.broadcasted_iota(jnp.int32, (t, LANES), 1)
    head_a = lane < ATTN_HEAD_DIM
    scale = ATTN_HEAD_DIM ** -0.5

    def unit(sn, hd):
        rows = slice(sn * t, (sn + 1) * t)
        sel = head_a if hd == 0 else jnp.logical_not(head_a)
        qm = jnp.where(sel, q_ref[rows, :] * scale, 0.0)
        s_c = _mm_nt(qm, kc_ref[sn]) + bc_ref[hd]
        s_n = _mm_nt(qm, kn_ref[rows, :]) + bn_ref[hd]
        yield
        return (yield from _softmax_pv([(s_c, vc_ref[sn]), (s_n, vn_ref[rows, :])]))

    for s0 in range(0, nstream, ATTN_CHUNKS_IN_FLIGHT):
        streams = range(s0, min(s0 + ATTN_CHUNKS_IN_FLIGHT, nstream))
        outs = _lockstep([unit(sn, hd) for sn in streams for hd in range(2)])
        for n, sn in enumerate(streams):
            o_ref[sn * t:(sn + 1) * t, :] = jnp.where(head_a, outs[2 * n], outs[2 * n + 1])


def _attn_cached(qkv, k_cache, v_cache, bias_c, bias_n, b, t):
    d = qkv.shape[1] // 3
    npair = d // LANES
    rows = k_cache.shape[1]
    ns = _streams_per_block(b, t, t)
    new = lambda col: pl.BlockSpec((ns * t, LANES), lambda bi, p: (bi, col * npair + p))
    cache = pl.BlockSpec((ns, rows, LANES), lambda bi, p: (bi, 0, p))
    return pl.pallas_call(
        functools.partial(_attn_cached_kernel, t=t, nstream=ns),
        grid=(b // ns, npair),
        in_specs=[
            new(0), new(1), new(2), cache, cache,
            pl.BlockSpec((2, t, rows), lambda bi, p: (p, 0, 0)),
            pl.BlockSpec((2, t, t), lambda bi, p: (p, 0, 0)),
        ],
        out_specs=new(0),
        out_shape=jax.ShapeDtypeStruct((b * t, d), F32),
        compiler_params=_params("parallel", "parallel"),
        name="attn_cached",
    )(qkv, qkv, qkv, k_cache, v_cache, bias_c, bias_n)


def _rel_bias(rel_bias, nq, nk, offset):
    rel = jnp.arange(nq + nk - 1) + (offset - nk + 1)
    diag = rel_bias[:, jnp.clip(rel, -REL_CLIP, REL_CLIP) + REL_CLIP].astype(F32)
    rev = diag[:, ::-1]
    return jnp.stack([rev[:, nq - 1 - q:nq - 1 - q + nk] for q in range(nq)], axis=1)


def _rwkv_rkv_kernel(h_ref, p_ref, mu_ref, w_ref, o_ref, l_ref):
    @pl.when(pl.program_id(2) == 0)
    def _():
        h = h_ref[...]
        l_ref[...] = (h + (p_ref[...] - h) * mu_ref[0]).astype(l_ref.dtype)

    o_ref[0] = jnp.dot(l_ref[...], w_ref[0], preferred_element_type=F32)


def _rwkv_rkv(h, h_prev, mu, w_rkv):
    t, d = h.shape
    tm, tn = _tile(t, 512), _tile(d, 2048)
    return pl.pallas_call(
        _rwkv_rkv_kernel,
        grid=(3, t // tm, d // tn),
        in_specs=[
            pl.BlockSpec((tm, d), lambda c, i, j: (i, 0)),
            pl.BlockSpec((tm, d), lambda c, i, j: (i, 0)),
            pl.BlockSpec((1, 1, d), lambda c, i, j: (c, 0, 0)),
            pl.BlockSpec((1, d, tn), lambda c, i, j: (c, 0, j)),
        ],
        out_specs=pl.BlockSpec((1, tm, tn), lambda c, i, j: (c, i, j)),
        out_shape=jax.ShapeDtypeStruct((3, t, d), F32),
        scratch_shapes=[pltpu.VMEM((tm, d), MXU_DTYPE)],
        compiler_params=_params("parallel", "parallel", "arbitrary"),
        name="rwkv_rkv",
    )(h, h_prev, mu.reshape(-1, 1, d), w_rkv)


def _softplus(y):
    return jnp.maximum(y, 0.0) + jnp.log1p(jnp.exp(-jnp.abs(y)))


def _rwkv_lora_kernel(h_ref, p_ref, mu_ref, w0_ref, w1_ref, w2_ref, a0_ref, a1_ref, a2_ref,
                      g1_ref, g2_ref, lw_ref, a_ref, gate_ref):
    h = h_ref[...]
    xx = p_ref[...] - h
    lerp = lambda c: h + xx * mu_ref[c]
    z = w0_ref[...] + _mm(jnp.tanh(_mm(lerp(0), w1_ref[...])), w2_ref[...])
    w_log = -_softplus(-z) - 0.5
    lw_ref[...] = -jnp.exp(w_log)
    a_ref[...] = _sigmoid(a0_ref[...] + _mm(_mm(lerp(1), a1_ref[...]), a2_ref[...]))
    gate_ref[...] = _mm(_sigmoid(_mm(lerp(2), g1_ref[...])), g2_ref[...])


def _rwkv_lora(h, h_prev, mu_wag, w0, w1, w2, a0, a1, a2, g1, g2):
    t, d = h.shape
    tm = _tile(t, 256)
    row = pl.BlockSpec((tm, d), lambda i: (i, 0))
    full = lambda a: pl.BlockSpec(a.shape, lambda i: (0,) * a.ndim)
    consts = [mu_wag.reshape(3, 1, d), w0.reshape(1, d), w1, w2, a0.reshape(1, d), a1, a2, g1, g2]
    return pl.pallas_call(
        _rwkv_lora_kernel,
        grid=(t // tm,),
        in_specs=[row, row] + [full(c) for c in consts],
        out_specs=[row, row, row],
        out_shape=[jax.ShapeDtypeStruct((t, d), F32)] * 3,
        compiler_params=_params("parallel"),
        name="rwkv_lora",
    )(h, h_prev, *consts)


def _rwkv_kernel(r_ref, k_ref, v_ref, lw_ref, a_ref, gate_ref, kkw_ref, kaw_ref, rkw_ref,
                 lng_ref, lnb_ref, s0_ref, o_ref, sout_ref, s_ref, *, c_rows, nchunk, npb, ngrp, nstream):
    i = pl.program_id(2)
    c2 = 2 * c_rows

    @pl.when(i == 0)
    def _():
        s_ref[...] = s0_ref[...]

    lane = lax.broadcasted_iota(jnp.int32, (c_rows, LANES), 1)
    head_a = lane < RWKV_HEAD
    si = lax.broadcasted_iota(jnp.int32, (c2, c2), 0)
    sj = lax.broadcasted_iota(jnp.int32, (c2, c2), 1)
    same_blk = (si // c_rows) == (sj // c_rows)
    strict = jnp.logical_and(same_blk, (sj % c_rows) < (si % c_rows))
    incl = jnp.logical_and(same_blk, (sj % c_rows) <= (si % c_rows))
    inv_n = 1.0 / RWKV_HEAD

    def stack(x):
        return jnp.concatenate([jnp.where(head_a, x, 0.0), jnp.where(head_a, 0.0, x)], axis=0)

    def head_sum(x):
        sa = jnp.sum(jnp.where(head_a, x, 0.0), axis=-1, keepdims=True)
        sb = jnp.sum(jnp.where(head_a, 0.0, x), axis=-1, keepdims=True)
        return jnp.where(head_a, sa, sb)

    def prepare(rows, lanes):
        r, kr, v = r_ref[rows, lanes], k_ref[rows, lanes], v_ref[rows, lanes]
        lw, a = lw_ref[rows, lanes], a_ref[rows, lanes]
        kkw, kaw = kkw_ref[:, lanes], kaw_ref[:, lanes]
        kk = kr * kkw
        ss = head_sum(kk * kk)
        cl = _cumsum_rows(lw)
        yield
        kk = kk * lax.rsqrt(jnp.maximum(ss, 1e-24))
        k = kr * (1.0 + (a - 1.0) * kaw)
        e_neg = jnp.exp(-cl)
        al = stack(-kk * jnp.exp(cl - lw))
        rt = stack(r * jnp.exp(cl))
        bg = stack(a * kk * e_neg)
        rhs = jnp.concatenate([bg, stack(k * e_neg)], axis=0)
        aa = _mm_nt(jnp.concatenate([al, rt], axis=0), rhs)
        yield
        n_ab = jnp.where(strict, aa[:c2, :c2], 0.0)
        a_ak = jnp.where(strict, aa[:c2, c2:], 0.0)
        a_rb = jnp.where(incl, aa[c2:, :c2], 0.0)
        a_rk = jnp.where(incl, aa[c2:, c2:], 0.0)
        vs = stack(v)
        x = jnp.concatenate([al, _mm(a_ak, vs)], axis=1)
        yield
        pw = n_ab
        x = x + _mm(pw, x)
        yield
        for _ in range(int(math.log2(c_rows)) - 1):
            pw = _mm(pw, pw)
            yield
            x = x + _mm(pw, x)
            yield
        w, u0 = x[:, :LANES], x[:, LANES:]
        y = _mm(a_rb, x)
        yield
        ro = rt + y[:, :LANES]
        o0 = y[:, LANES:] + _mm(a_rk, vs)
        yield
        p = _mm_tn(w, bg)
        yield
        q = _mm_tn(jnp.concatenate([u0, vs], axis=0), rhs)
        yield
        gam = jnp.exp(cl[c_rows - 1:c_rows, :])
        bonus = head_sum(r * k * rkw_ref[:, lanes]) * v
        return ro, o0, p, q, gam, bonus

    def finish(rows, lanes, o2, bonus):
        o = o2[:c_rows] + o2[c_rows:]
        mean = head_sum(o) * inv_n
        dlt = o - mean
        var = head_sum(dlt * dlt) * inv_n
        on =dlt * lax.rsqrt(var + RWKV_GN_EPS) * lng_ref[:, lanes] + lnb_ref[:, lanes]
        o_ref[rows, lanes] = (on + bonus) * gate_ref[rows, lanes]

    def group(g, carry):
        r0 = pl.multiple_of(g * (ngrp * c_rows), ngrp * c_rows)
        lanes = [slice(pi * LANES, (pi + 1) * LANES) for pi in range(npb)]
        rows = {(sn, c): pl.ds(sn * (nchunk * c_rows) + r0 + c * c_rows, c_rows)
                for sn in range(nstream) for c in range(ngrp)}
        units = [(sn, c, pi) for c in range(ngrp) for sn in range(nstream) for pi in range(npb)]
        prep = dict(zip(units, _lockstep([prepare(rows[sn, c], lanes[pi]) for sn, c, pi in units])))
        s = {(sn, pi): s_ref[sn, pi] for sn in range(nstream) for pi in range(npb)}
        for sn, c, pi in units:
            ro, o0, p, q, gam, bonus = prep[sn, c, pi]
            finish(rows[sn, c], lanes[pi], _mm_nt(ro, s[sn, pi]) + o0, bonus)
            s[sn, pi] = (s[sn, pi] + _mm(s[sn, pi], p) + q) * gam
        for (sn, pi), val in s.items():
            s_ref[sn, pi] = val
        return carry

    lax.fori_loop(0, nchunk // ngrp, group, 0)

    @pl.when(i == pl.num_programs(2) - 1)
    def _():
        sout_ref[...] = s_ref[...]


def _rwkv_recurrence(rkv, lw, a, gate, kkw, kaw, rkw, lng, lnb, s0_blk, b, t):
    d = lw.shape[1]
    npair = d // LANES
    c_rows = min(CHUNK, t)
    tb = _tile(t, 512)
    nb = t // tb
    nchunk = tb // c_rows
    ns = _streams_per_block(b, t, tb)
    npb = 2 if npair % 2 == 0 else 1
    ngrp = _tile(nchunk, RWKV_CHUNKS_IN_FLIGHT)
    bw = npb * LANES
    kern = functools.partial(_rwkv_kernel, c_rows=c_rows, nchunk=nchunk, npb=npb, ngrp=ngrp, nstream=ns)
    rowblk = pl.BlockSpec((ns * tb, bw), lambda bi, p, i: (bi * nb + i, p))
    rkvblk = lambda c: pl.BlockSpec((None, ns * tb, bw), lambda bi, p, i: (c, bi * nb + i, p))
    vec = pl.BlockSpec((1, bw), lambda bi, p, i: (0, p))
    st = pl.BlockSpec((ns, npb, LANES, LANES), lambda bi, p, i: (bi, p, 0, 0))
    return pl.pallas_call(
        kern,
        grid=(b // ns, npair // npb, nb),
        in_specs=[rkvblk(0), rkvblk(1), rkvblk(2), rowblk, rowblk, rowblk, vec, vec, vec, vec, vec, st],
        out_specs=[rowblk, st],
        out_shape=[jax.ShapeDtypeStruct((b * t, d), F32),
                   jax.ShapeDtypeStruct((b, npair, LANES, LANES), F32)],
        scratch_shapes=[pltpu.VMEM((ns, npb, LANES, LANES), F32)],
        compiler_params=_params("parallel", "parallel", "arbitrary"),
        name="rwkv_recurrence",
    )(rkv, rkv, rkv, lw, a, gate, kkw.reshape(1, d), kaw.reshape(1, d), rkw.reshape(1, d),
      lng.reshape(1, d), lnb.reshape(1, d), s0_blk)


def _to_blockdiag(s):
    b, h, n, _ = s.shape
    s = s.reshape(b, h // 2, 2, n, n)
    z = jnp.zeros_like(s[:, :, 0])
    top = jnp.concatenate([s[:, :, 0], z], axis=-1)
    bot = jnp.concatenate([z, s[:, :, 1]], axis=-1)
    return jnp.concatenate([top, bot], axis=-2)


def _from_blockdiag(sb):
    b, p, n2, _ = sb.shape
    n = n2 // 2
    return jnp.stack([sb[:, :, :n, :n], sb[:, :, n:, n:]], axis=2).reshape(b, 2 * p, n, n)


def _trunk(x3, hgrn_s0, k_cache, v_cache, wkv_s0, shift0, w, lower):
    b, t, d = x3.shape
    x = x3.reshape(b * t, d)
    depth = w['norm_g'].shape[0]
    hgrn_out, k_out, v_out, wkv_out, shift_out = [], [], [], [], []
    xn = _norm(x, w['norm_g'][0, 0], MXU_DTYPE)
    for layer in range(depth):
        kind, j = layer % 3, layer // 3
        g = w['norm_g'][layer]
        last = layer == depth - 1
        ffn = w['ffn'][2 * layer:2 * layer + 3] + [None]
        x, n = _ffn(x, xn, ffn[0], ffn[1], g[1], F32 if kind == 2 else MXU_DTYPE)
        if kind == 0:
            proj = _proj(n, w['hgrn_w_in'][j])
            o, s_fin = _hgrn_recurrence(proj, lower[layer], hgrn_s0[j], b, t)
            hgrn_out.append(s_fin)
            x, xn = _gated_proj_res(x, o, proj, 3, w['hgrn_norm_g'][j], w['hgrn_w_out'][j], g[2], MXU_DTYPE)
        elif kind == 1:
            qkv = _proj(n, w['attn_w_qkv'][j])
            rel_bias = w['attn_rel_bias'][j]
            nh = rel_bias.shape[0]
            if k_cache is None:
                past = PREV_CHUNKS * CHUNK
                o = _attn_prompt(qkv, _rel_bias(rel_bias, CHUNK, BAND, past), b, t)
                kept = min(past, t)
            else:
                rows = k_cache.shape[2]
                bias = _rel_bias(rel_bias, t, rows + t, rows)
                o = _attn_cached(qkv, k_cache[j].reshape(b, rows, d), v_cache[j].reshape(b, rows, d),
                                 bias[:, :, :rows], bias[:, :, rows:], b, t)
                kept = t
            qkv3 = qkv.reshape(b, t, 3 * d)
            keep = lambda c: qkv3[:, t - kept:, c * d:(c + 1) * d].reshape(b, kept, nh, ATTN_HEAD_DIM)
            k_out.append(keep(1))
            v_out.append(keep(2))
            x, xn = _proj_res(x, o, w['attn_w_out'][j], g[2], MXU_DTYPE)
        else:
            h = n
            h3 = h.reshape(b, t, d)
            h_prev = jnp.concatenate([shift0[j].astype(F32), h3[:, :-1]], axis=1).reshape(b * t, d)
            mu = w['rwkv_mu'][j]
            rkv = _rwkv_rkv(h, h_prev, mu[:3], w['rwkv_w_rkv'][j])
            lw, a, gate = _rwkv_lora(h, h_prev, mu[3:], w['rwkv_w0'][j], w['rwkv_w1'][j], w['rwkv_w2'][j],
                                     w['rwkv_a0'][j], w['rwkv_a1'][j], w['rwkv_a2'][j],
                                     w['rwkv_g1'][j], w['rwkv_g2'][j])
            o, s_blk = _rwkv_recurrence(rkv, lw, a, gate, w['rwkv_k_k'][j], w['rwkv_k_a'][j],
                                        w['rwkv_r_k'][j].reshape(-1), w['rwkv_ln_g'][j], w['rwkv_ln_b'][j],
                                        _to_blockdiag(wkv_s0[j]), b, t)
            shift_out.append(h3[:, -1:])
            wkv_out.append(_from_blockdiag(s_blk))
            x, xn = _proj_res(x, o, w['rwkv_w_out'][j], g[2], MXU_DTYPE)
        g_next = w['final_norm_g'] if last else w['norm_g'][layer + 1, 0]
        x, xn = _ffn(x, xn, ffn[1], ffn[2], g_next, F32 if last else MXU_DTYPE, keep_x=not last)
    y = xn.reshape(b, t, d)
    return (y,jnp.stack(hgrn_out), jnp.stack(k_out), jnp.stack(v_out),
            jnp.stack(wkv_out), jnp.stack(shift_out))


def kernel(x_prompt, x_sample, state_hgrn, cache_k_band, cache_v_band, state_wkv, state_shift,
           norm_g, final_norm_g, ffn_w_gate_up, ffn_w_down,
           hgrn_w_in, hgrn_lb_logits, hgrn_norm_g, hgrn_w_out,
           attn_w_qkv, attn_rel_bias, attn_w_out,
           rwkv_mu, rwkv_w_rkv, rwkv_w0, rwkv_w1, rwkv_w2, rwkv_a0, rwkv_a1, rwkv_a2,
           rwkv_g1, rwkv_g2, rwkv_k_k, rwkv_k_a, rwkv_r_k, rwkv_ln_g, rwkv_ln_b, rwkv_w_out):
    depth = norm_g.shape[0]
    cast = lambda a: a.astype(MXU_DTYPE)
    ffn = [dict(gu_all=ffn_w_gate_up, d_all=ffn_w_down, idx=(l, i), gate=None, up=None, down=None)
           for l in range(depth) for i in range(2)]
    w = dict(norm_g=norm_g, final_norm_g=final_norm_g, ffn=ffn,
             hgrn_w_in=cast(hgrn_w_in), hgrn_norm_g=hgrn_norm_g, hgrn_w_out=cast(hgrn_w_out),
             attn_w_qkv=cast(attn_w_qkv), attn_rel_bias=attn_rel_bias, attn_w_out=cast(attn_w_out),
             rwkv_mu=rwkv_mu, rwkv_w_rkv=cast(rwkv_w_rkv), rwkv_w0=rwkv_w0, rwkv_w1=cast(rwkv_w1),
             rwkv_w2=cast(rwkv_w2), rwkv_a0=rwkv_a0, rwkv_a1=cast(rwkv_a1), rwkv_a2=cast(rwkv_a2),
             rwkv_g1=cast(rwkv_g1), rwkv_g2=cast(rwkv_g2), rwkv_k_k=rwkv_k_k, rwkv_k_a=rwkv_k_a,
             rwkv_r_k=rwkv_r_k, rwkv_ln_g=rwkv_ln_g, rwkv_ln_b=rwkv_ln_b, rwkv_w_out=cast(rwkv_w_out))
    probs = jax.nn.softmax(hgrn_lb_logits.astype(F32), axis=0)
    lower = jnp.cumsum(probs, axis=0) - probs[0]

    b = x_prompt.shape[0]
    d = x_prompt.shape[2]
    n_a, n_c = state_hgrn.shape[0], state_wkv.shape[0]
    hgrn0 = jnp.zeros((n_a, b) + state_hgrn.shape[2:], F32)
    wkv0 = jnp.zeros((n_c, b) + state_wkv.shape[2:], F32)
    shift0 = jnp.zeros((n_c, b, 1, d), F32)
    y_p, hgrn_p, k_p, v_p, wkv_p, shift_p = _trunk(x_prompt, hgrn0, None, None, wkv0, shift0, w, lower)
    y_s, hgrn_s, k_s, v_s, wkv_s, shift_s = _trunk(x_sample, state_hgrn, cache_k_band, cache_v_band,
                                                   state_wkv, state_shift, w, lower)
    return (y_p, y_s, hgrn_p, hgrn_s, k_p, v_p, k_s, v_s, wkv_p, wkv_s, shift_p, shift_s)
```
